```python
import jax, jax.numpy as jnp
from jax import lax
import numpy as np

D_MODEL = 1024
BATCH = 2
SEQ = 8192
DEPTH = 2
DEC_BATCH = 128
DEC_SEQ = 4
PAST_LEN = 8192
PAGE_SIZE = 128

N_EVEN = (DEPTH + 1) // 2
N_ODD = DEPTH // 2
NORM_EPS = 1e-6

MLA_HEADS = 8
Q_LORA = 384
KV_LORA = 256
QK_NOPE = 64
QK_ROPE = 32
V_HEAD = 64
ROPE_THETA = 10000.0
Q_BLOCK = 128

ML_HEADS = 4
ML_DQK = 128
ML_DV = 128
ML_CHUNK = 128
M_INIT = -1e30

IN_WIDTHS = (Q_LORA, KV_LORA + QK_ROPE, ML_HEADS * ML_DQK, ML_HEADS * ML_DQK,
             ML_HEADS * ML_DV, ML_HEADS, ML_HEADS, ML_HEADS * ML_DV)
IN_DIM = sum(IN_WIDTHS)
IN_SPLITS = tuple(int(c) for c in np.cumsum(IN_WIDTHS)[:-1])
MIX_DIM = MLA_HEADS * V_HEAD + ML_HEADS * ML_DV

CONV_W = 31

PEER_HEADS = 8
N_KEYS = 128
N_EXPERTS = N_KEYS * N_KEYS
PEER_DKEY = 256
PEER_TOPK = 16
PEER_BLOCK = 256

kernel_name = 'hybrid_mlstm_mla_conformer_peer_step'


def rmsnorm(x, g):
    xf = x.astype(jnp.float32)
    y = xf * lax.rsqrt(jnp.mean(xf * xf, axis=-1, keepdims=True) + NORM_EPS)
    return (y * g.astype(jnp.float32)).astype(x.dtype)


def layernorm(x, g, b):
    xf = x.astype(jnp.float32)
    mu = jnp.mean(xf, axis=-1, keepdims=True)
    var = jnp.mean(jnp.square(xf - mu), axis=-1, keepdims=True)
    y = (xf - mu) * lax.rsqrt(var + NORM_EPS)
    return (y * g.astype(jnp.float32) + b.astype(jnp.float32)).astype(x.dtype)


def rope(x, pos):
    half = QK_ROPE // 2
    inv = ROPE_THETA ** (-jnp.arange(half, dtype=jnp.float32) / half)
    ang = pos.astype(jnp.float32)[:, None] * inv[None, :]
    ang = ang.reshape((ang.shape[0],) + (1,) * (x.ndim - 3) + (half,))
    cos, sin = jnp.cos(ang), jnp.sin(ang)
    xf = x.astype(jnp.float32)
    x1, x2 = xf[..., :half], xf[..., half:]
    return jnp.concatenate([x1 * cos - x2 * sin, x1 * sin + x2 * cos], axis=-1).astype(x.dtype)


def even_project(h, pos, w_in, q_norm, w_uq, kv_norm, b_i, b_f):
    B, S, _ = h.shape
    f32 = jnp.float32
    z = h @ w_in
    cq, ckv, mq, mk, mv, mi, mf, mo = jnp.split(z, IN_SPLITS, axis=-1)
    q = (rmsnorm(cq, q_norm) @ w_uq).reshape(B, S, MLA_HEADS, QK_NOPE + QK_ROPE)
    q_nope = q[..., :QK_NOPE]
    q_rope = rope(q[..., QK_NOPE:], pos)
    c_kv = rmsnorm(ckv[..., :KV_LORA], kv_norm)
    k_rope = rope(ckv[..., KV_LORA:], pos)
    mq = mq.reshape(B, S, ML_HEADS, ML_DQK).astype(f32)
    mk = mk.reshape(B, S, ML_HEADS, ML_DQK).astype(f32) * (ML_DQK ** -0.5)
    mv = mv.reshape(B, S, ML_HEADS, ML_DV).astype(f32)
    i_pre = mi.astype(f32) + b_i.astype(f32)
    f_pre = mf.astype(f32) + b_f.astype(f32)
    return q_nope, q_rope, c_kv, k_rope, mq, mk, mv, i_pre, f_pre, mo


def mla_prompt(q_nope, q_rope, c_kv, k_rope, w_uk, w_uv):
    S = q_nope.shape[1]
    scale = (QK_NOPE + QK_ROPE) ** -0.5
    k_nope = jnp.einsum('bsc,chd->bshd', c_kv, w_uk)
    v = jnp.einsum('bsc,chd->bshd', c_kv, w_uv)
    outs = []
    for qb in range(S // Q_BLOCK):
        lo, hi = qb * Q_BLOCK, (qb + 1) * Q_BLOCK
        s = (jnp.einsum('bqhd,bkhd->bhqk', q_nope[:, lo:hi], k_nope[:, :hi])
             + jnp.einsum('bqhr,bkr->bhqk', q_rope[:, lo:hi], k_rope[:, :hi])).astype(jnp.float32) * scale
        mask = (lo + jnp.arange(Q_BLOCK))[:, None] >= jnp.arange(hi)[None, :]
        p = jax.nn.softmax(jnp.where(mask, s, -jnp.inf), axis=-1).astype(v.dtype)
        outs.append(jnp.einsum('bhqk,bkhd->bqhd', p, v[:, :hi]))
    return jnp.concatenate(outs, axis=1)


def mla_sample(q_nope, q_rope, c_new, kr_new, c_past, kr_past, w_uk, w_uv):
    T = q_nope.shape[1]
    P = c_past.shape[1]
    scale = (QK_NOPE + QK_ROPE) ** -0.5
    q_lat = jnp.einsum('bthd,chd->bthc', q_nope, w_uk)
    s_past = (jnp.einsum('bthc,bkc->bhtk', q_lat, c_past)
              + jnp.einsum('bthr,bkr->bhtk', q_rope, kr_past)).astype(jnp.float32) * scale
    s_new = (jnp.einsum('bthc,bkc->bhtk', q_lat, c_new)
             + jnp.einsum('bthr,bkr->bhtk', q_rope, kr_new)).astype(jnp.float32) * scale
    causal = jnp.tril(jnp.ones((T, T), dtype=bool))
    s_new = jnp.where(causal, s_new, -jnp.inf)
    p = jax.nn.softmax(jnp.concatenate([s_past, s_new], axis=-1), axis=-1).astype(c_new.dtype)
    o_lat = (jnp.einsum('bhtk,bkc->bthc', p[..., :P], c_past)
             + jnp.einsum('bhtk,bkc->bthc', p[..., P:], c_new))
    return jnp.einsum('bthc,chd->bthd', o_lat, w_uv)


def mlstm_chunked(q, k, v, i_pre, f_pre, C0, n0, m0):
    B, S, H, _ = q.shape
    L = ML_CHUNK if S % ML_CHUNK == 0 else S
    nc = S // L
    logf = jax.nn.log_sigmoid(f_pre)

    def chunks(a):
        return jnp.moveaxis(a.reshape((B, nc, L) + a.shape[2:]), 1, 0)

    causal = jnp.tril(jnp.ones((L, L), dtype=bool))[None, :, :, None]

    def step(carry, inp):
        C, n, m = carry
        qc, kc, vc, ic, lfc = inp
        b = jnp.cumsum(lfc, axis=1)
        a = ic - b
        m_t = b + jnp.maximum(m[:, None, :], lax.cummax(a, axis=1))
        d_log = (b - m_t)[:, :, None, :] + a[:, None, :, :]
        w = jnp.exp(jnp.where(causal, d_log, -jnp.inf))
        inter = jnp.exp(m[:, None, :] + b - m_t)
        wqk = w * jnp.einsum('bthd,bshd->btsh', qc, kc)
        num = (jnp.einsum('btsh,bshv->bthv', wqk, vc)
               + inter[..., None] * jnp.einsum('bthd,bhdv->bthv', qc, C))
        den = wqk.sum(axis=2) + inter * jnp.einsum('bthd,bhd->bth', qc, n)
        h = num / jnp.maximum(jnp.abs(den), jnp.exp(-m_t))[..., None]
        m_last = m_t[:, -1]
        w_s = jnp.exp(a + b[:, -1:, :] - m_last[:, None, :])
        f_last = jnp.exp(m + b[:, -1] - m_last)
        C_new = f_last[..., None, None] * C + jnp.einsum('bsh,bshd,bshv->bhdv', w_s, kc, vc)
        n_new = f_last[..., None] * n + jnp.einsum('bsh,bshd->bhd', w_s, kc)
        return (C_new, n_new, m_last), h

    (C_f, n_f, m_f), hs = lax.scan(step, (C0, n0, m0),
                                   (chunks(q), chunks(k), chunks(v), chunks(i_pre), chunks(logf)))
    h = jnp.moveaxis(hs, 0, 1).reshape(B, S, H, v.shape[-1])
    return h, C_f, n_f, m_f


def even_merge(att, h_ml, o_pre, ml_norm, w_out):
    B, S = att.shape[0], att.shape[1]
    hn = h_ml * lax.rsqrt(jnp.mean(h_ml * h_ml, axis=-1, keepdims=True) + NORM_EPS)
    y_ml = (hn.reshape(B, S, ML_HEADS * ML_DV) * ml_norm.astype(jnp.float32)
            * jax.nn.sigmoid(o_pre.astype(jnp.float32))).astype(att.dtype)
    mix = jnp.concatenate([att.reshape(B, S, MLA_HEADS * V_HEAD), y_ml], axis=-1)
    return mix @ w_out


def conformer_conv(h, buf, w_pw1, b_pw1, w_dw, b_dw, ln_g, ln_b, w_pw2, b_pw2):
    a = h @ w_pw1 + b_pw1
    g = a[..., :D_MODEL] * jax.nn.sigmoid(a[..., D_MODEL:])
    full = jnp.concatenate([buf.astype(g.dtype), g], axis=1)
    y = lax.conv_general_dilated(full, w_dw[:, None, :].astype(full.dtype), window_strides=(1,),
                                 padding='VALID', dimension_numbers=('NWC', 'WIO', 'NWC'),
                                 feature_group_count=D_MODEL) + b_dw
    y = jax.nn.silu(layernorm(y, ln_g, ln_b))
    return y @ w_pw2 + b_pw2, full[:, -(CONV_W - 1):]


def peer(h, w_q, sub_keys, u_tab, v_tab):
    B, S, D = h.shape
    t = h.reshape(B * S, D)
    n = t.shape[0]
    pad = (-n) % PEER_BLOCK
    t = jnp.pad(t, ((0, pad), (0, 0))).reshape(-1, PEER_BLOCK, D)

    def block(tb):
        q = (tb @ w_q).reshape(PEER_BLOCK, PEER_HEADS, 2, PEER_DKEY // 2).astype(jnp.float32)
        s = jnp.einsum('thpd,hpkd->thpk', q, sub_keys.astype(jnp.float32))
        v1, i1 = lax.top_k(s[:, :, 0], PEER_TOPK)
        v2, i2 = lax.top_k(s[:, :, 1], PEER_TOPK)
        comb = (v1[..., :, None] + v2[..., None, :]).reshape(PEER_BLOCK, PEER_HEADS, PEER_TOPK * PEER_TOPK)
        sc, ci = lax.top_k(comb, PEER_TOPK)
        e1 = jnp.take_along_axis(i1, ci // PEER_TOPK, axis=-1)
        e2 = jnp.take_along_axis(i2, ci % PEER_TOPK, axis=-1)
        eid = e1 * N_KEYS + e2
        gate = jax.nn.softmax(sc, axis=-1)
        act = jax.nn.gelu(jnp.einsum('td,thkd->thk', tb, u_tab[eid]).astype(jnp.float32), approximate=False)
        wgt = (gate * act).astype(tb.dtype)
        return jnp.einsum('thk,thkd->td', wgt, v_tab[eid])

    out = lax.map(block, t).reshape(-1, D)[:n]
    return out.reshape(B, S, D)


def setup_inputs(seed: int = 0) -> dict:
    key = jax.random.key(seed)
    keys = iter(jax.random.split(key, 64))
    f32 = jnp.float32
    D = D_MODEL

    def nrm(shape, scale):
        return jax.random.normal(next(keys), shape, f32) * scale

    def gain(shape):
        return 1.0 + 0.02 * jax.random.normal(next(keys), shape, f32)

    n_pages = PAST_LEN // PAGE_SIZE
    n_used = DEC_BATCH * n_pages
    n_pool = n_used + n_used // 4
    inp = {}
    inp['x_prompt'] = nrm((BATCH, SEQ, D), 1.0)
    inp['x_sample'] = nrm((DEC_BATCH, DEC_SEQ, D), 1.0)
    inp['cache_mla_latent'] = nrm((N_EVEN, n_pool, PAGE_SIZE, KV_LORA), 1.0)
    inp['cache_mla_rope'] = nrm((N_EVEN, n_pool, PAGE_SIZE, QK_ROPE), 1.0)
    inp['state_mlstm_C'] = nrm((N_EVEN, DEC_BATCH, ML_HEADS, ML_DQK, ML_DV), 0.1)
    inp['state_mlstm_n'] = nrm((N_EVEN, DEC_BATCH, ML_HEADS, ML_DQK), 0.1)
    inp['state_mlstm_m'] = nrm((N_EVEN, DEC_BATCH, ML_HEADS), 1.0)
    inp['state_conv'] = nrm((N_ODD, DEC_BATCH, CONV_W - 1, D), 0.5)
    inp['page_table'] = jax.random.permutation(next(keys), n_pool)[:n_used].reshape(DEC_BATCH, n_pages).astype(jnp.int32)
    inp['attn_norm'] = gain((N_EVEN, D))
    inp['w_in'] = nrm((N_EVEN, D, IN_DIM), D ** -0.5)
    inp['q_norm'] = gain((N_EVEN, Q_LORA))
    inp['w_uq'] = nrm((N_EVEN, Q_LORA, MLA_HEADS * (QK_NOPE + QK_ROPE)), Q_LORA ** -0.5)
    inp['kv_norm'] = gain((N_EVEN, KV_LORA))
    inp['w_uk'] = nrm((N_EVEN, KV_LORA, MLA_HEADS, QK_NOPE), KV_LORA ** -0.5)
    inp['w_uv'] = nrm((N_EVEN, KV_LORA, MLA_HEADS, V_HEAD), KV_LORA ** -0.5)
    inp['b_igate'] = nrm((N_EVEN, ML_HEADS), 0.1)
    inp['b_fgate'] = 3.0 + nrm((N_EVEN, ML_HEADS), 0.5)
    inp['ml_norm'] = gain((N_EVEN, ML_HEADS * ML_DV))
    inp['w_out'] = nrm((N_EVEN, MIX_DIM, D), MIX_DIM ** -0.5)
    inp['conv_norm'] = gain((N_ODD, D))
    inp['w_pw1'] = nrm((N_ODD, D, 2 * D), D ** -0.5)
    inp['b_pw1'] = nrm((N_ODD, 2 * D), 0.02)
    inp['w_dw'] = nrm((N_ODD, CONV_W, D), CONV_W ** -0.5)
    inp['b_dw'] = nrm((N_ODD, D), 0.02)
    inp['conv_ln_g'] = gain((N_ODD, D))
    inp['conv_ln_b'] = nrm((N_ODD, D), 0.02)
    inp['w_pw2'] = nrm((N_ODD, D, D), D ** -0.5)
    inp['b_pw2'] = nrm((N_ODD, D), 0.02)
    inp['ffn_norm'] = gain((DEPTH, D))
    inp['peer_wq'] = nrm((DEPTH, D, PEER_HEADS * PEER_DKEY), D ** -0.5)
    inp['peer_keys'] = nrm((DEPTH, PEER_HEADS, 2, N_KEYS, PEER_DKEY // 2), (PEER_DKEY // 2) ** -0.5)
    inp['peer_u'] = nrm((DEPTH, N_EXPERTS, D), D ** -0.5)
    inp['peer_v'] = nrm((DEPTH, N_EXPERTS, D), PEER_HEADS ** -0.5)
    inp['final_norm'] = gain((D,))
    return inp


def reference(x_prompt, x_sample, cache_mla_latent, cache_mla_rope, state_mlstm_C, state_mlstm_n,
              state_mlstm_m, state_conv, page_table, attn_norm, w_in, q_norm, w_uq, kv_norm, w_uk,
              w_uv, b_igate, b_fgate, ml_norm, w_out, conv_norm, w_pw1, b_pw1, w_dw, b_dw,
              conv_ln_g, conv_ln_b, w_pw2, b_pw2, ffn_norm, peer_wq, peer_keys, peer_u, peer_v,
              final_norm):
    f32 = jnp.float32
    bp, sp = x_prompt.shape[0], x_prompt.shape[1]
    bs, ss = x_sample.shape[0], x_sample.shape[1]
    pos_p = jnp.arange(sp, dtype=jnp.int32)
    pos_s = PAST_LEN + jnp.arange(ss, dtype=jnp.int32)
    past = page_table.shape[1] * PAGE_SIZE
    xp, xs = x_prompt, x_sample
    lat_p, rope_p, lat_s, rope_s = [], [], [], []
    cp_l, np_l, mp_l, cs_l, ns_l, ms_l = [], [], [], [], [], []
    convp_l, convs_l = [], []
    for layer in range(DEPTH):
        j = layer // 2
        if layer % 2 == 0:
            (qn_p, qr_p, c_p, kr_p, mq_p, mk_p, mv_p, ig_p, fg_p, og_p) = even_project(
                rmsnorm(xp, attn_norm[j]), pos_p, w_in[j], q_norm[j], w_uq[j], kv_norm[j], b_igate[j], b_fgate[j])
            (qn_s, qr_s, c_s, kr_s, mq_s, mk_s, mv_s, ig_s, fg_s, og_s) = even_project(
                rmsnorm(xs, attn_norm[j]), pos_s, w_in[j], q_norm[j], w_uq[j], kv_norm[j], b_igate[j], b_fgate[j])
            att_p = mla_prompt(qn_p, qr_p, c_p, kr_p, w_uk[j], w_uv[j])
            c_past = cache_mla_latent[j][page_table].reshape(bs, past, KV_LORA)
            kr_past = cache_mla_rope[j][page_table].reshape(bs, past, QK_ROPE)
            att_s = mla_sample(qn_s, qr_s, c_s, kr_s, c_past, kr_past, w_uk[j], w_uv[j])
            C0 = jnp.zeros((bp, ML_HEADS, ML_DQK, ML_DV), f32)
            n0 = jnp.zeros((bp, ML_HEADS, ML_DQK), f32)
            m0 = jnp.full((bp, ML_HEADS), M_INIT, f32)
            h_p, C_p, n_p, m_p = mlstm_chunked(mq_p, mk_p, mv_p, ig_p, fg_p, C0, n0, m0)
            h_s, C_s, n_s, m_s = mlstm_chunked(mq_s, mk_s, mv_s, ig_s, fg_s,
                                               state_mlstm_C[j].astype(f32), state_mlstm_n[j].astype(f32),
                                               state_mlstm_m[j].astype(f32))
            xp = xp + even_merge(att_p, h_p, og_p, ml_norm[j], w_out[j])
            xs = xs + even_merge(att_s, h_s, og_s, ml_norm[j], w_out[j])
            lat_p.append(c_p)
            rope_p.append(kr_p)
            lat_s.append(c_s)
            rope_s.append(kr_s)
            cp_l.append(C_p)
            np_l.append(n_p)
            mp_l.append(m_p)
            cs_l.append(C_s)
            ns_l.append(n_s)
            ms_l.append(m_s)
        else:
            buf0 = jnp.zeros((bp, CONV_W - 1, D_MODEL), xp.dtype)
            out_p, buf_p = conformer_conv(rmsnorm(xp, conv_norm[j]), buf0, w_pw1[j], b_pw1[j], w_dw[j], b_dw[j],
                                          conv_ln_g[j], conv_ln_b[j], w_pw2[j], b_pw2[j])
            out_s, buf_s = conformer_conv(rmsnorm(xs, conv_norm[j]), state_conv[j], w_pw1[j], b_pw1[j], w_dw[j], b_dw[j],
                                          conv_ln_g[j], conv_ln_b[j], w_pw2[j], b_pw2[j])
            xp = xp + out_p
            xs = xs + out_s
            convp_l.append(buf_p)
            convs_l.append(buf_s)
        xp = xp + peer(rmsnorm(xp, ffn_norm[layer]), peer_wq[layer], peer_keys[layer], peer_u[layer], peer_v[layer])
        xs = xs + peer(rmsnorm(xs, ffn_norm[layer]), peer_wq[layer], peer_keys[layer], peer_u[layer], peer_v[layer])
    y_prompt = rmsnorm(xp, final_norm)
    y_sample = rmsnorm(xs, final_norm)
    latent_prompt = jnp.stack(lat_p)
    rope_prompt = jnp.stack(rope_p)
    latent_sample = jnp.stack(lat_s)
    rope_sample = jnp.stack(rope_s)
    mlstm_C_prompt = jnp.stack(cp_l)
    mlstm_n_prompt = jnp.stack(np_l)
    mlstm_m_prompt = jnp.stack(mp_l)
    mlstm_C_sample = jnp.stack(cs_l)
    mlstm_n_sample = jnp.stack(ns_l)
    mlstm_m_sample = jnp.stack(ms_l)
    conv_prompt = jnp.stack(convp_l)
    conv_sample = jnp.stack(convs_l)
    return (y_prompt, y_sample, latent_prompt, rope_prompt, latent_sample, rope_sample,
            mlstm_C_prompt, mlstm_n_prompt, mlstm_m_prompt, mlstm_C_sample, mlstm_n_sample, mlstm_m_sample,
            conv_prompt, conv_sample)
```

```python
import functools
import math

import jax
import jax.numpy as jnp
import numpy as np
from jax import lax
from jax.experimental import pallas as pl
from jax.experimental.pallas import tpu as pltpu

D_MODEL = 1024
DEPTH = 2
PAST_LEN = 8192
PAGE_SIZE = 128
NORM_EPS = 1e-6

MLA_HEADS = 8
Q_LORA = 384
KV_LORA = 256
QK_NOPE = 64
QK_ROPE = 32
V_HEAD = 64
ROPE_THETA = 10000.0
Q_BLOCK = 128

ML_HEADS = 4
ML_DQK = 128
ML_DV = 128
ML_CHUNK = 128
M_INIT = -1e30

IN_WIDTHS = (Q_LORA, KV_LORA + QK_ROPE, ML_HEADS * ML_DQK, ML_HEADS * ML_DQK,
             ML_HEADS * ML_DV, ML_HEADS, ML_HEADS, ML_HEADS * ML_DV)
IN_SPLITS = tuple(int(c) for c in np.cumsum(IN_WIDTHS)[:-1])

CONV_W = 31

PEER_HEADS = 8
N_KEYS = 128
N_EXPERTS = N_KEYS * N_KEYS
PEER_DKEY = 256
PEER_HALF = PEER_DKEY // 2
PEER_TOPK = 16

BF16_ROWS = 16
PEER_TOKENS = 512
PEER_EXPERT_CHUNK = 1024
PEER_VMEM_LIMIT = 56 * 1024 * 1024

NEG_INF = float("-inf")
SQRT_HALF = math.sqrt(0.5)


def _peer_candidates(v1, v2):
    t = v1.shape[1]
    row = lax.broadcasted_iota(jnp.int32, (8, t), 0)
    slabs = [v1[0:8] + v2[0:1], v1[8:16] + v2[0:1], v1[0:8] + v2[1:2]]
    for b in range(2, 8):
        n_a = PEER_TOPK // (b + 1)
        slabs.append(jnp.where(row < n_a, v1[0:8] + v2[b:b + 1], NEG_INF))
    slabs.append(v1[0:1] + v2[8:16])
    return jnp.concatenate(slabs, axis=0)


def _extract_top(cur, n, out_ref=None, head=None):
    vals = []
    for i in range(n):
        m = jnp.max(cur, axis=0, keepdims=True)
        vals.append(m)
        if i + 1 < n:
            cur = jnp.where(cur == m, NEG_INF, cur)
    return vals


def _peer_kernel(x_ref, g_ref, wqt_ref, keys_ref, u_ref, vt_ref, o_ref,
                 ht_s, s1_s, s2_s, e1_s, e2_s, tau_s, top_s, at_s, coef_s, acc_s):
    j = pl.program_id(1)
    n_chunks = pl.num_programs(1)
    tb = x_ref.shape[0]
    chunk = u_ref.shape[0]
    rows_per_chunk = chunk // N_KEYS

    @pl.when(j == 0)
    def _prologue():
        x = x_ref[...]
        ms = jnp.mean(x * x, axis=-1, keepdims=True)
        hn = (x * lax.rsqrt(ms + NORM_EPS)) * g_ref[...]
        ht = hn.T.astype(jnp.bfloat16)
        ht_s[...] = ht
        qt = jnp.dot(wqt_ref[...], ht, preferred_element_type=jnp.float32)
        qtb = qt.astype(jnp.bfloat16)
        for h in range(PEER_HEADS):
            tops = []
            for p in range(2):
                r0 = (h * 2 + p) * PEER_HALF
                s = jnp.dot(keys_ref[h, p], qtb[r0:r0 + PEER_HALF, :],
                            preferred_element_type=jnp.float32)
                if p == 0:
                    s1_s[h] = s
                else:
                    s2_s[h] = s
                vals = _extract_top(s, PEER_TOPK)
                for i, m in enumerate(vals):
                    top_s[p, i:i + 1, :] = m
                tops.append(vals[0])
            v1 = top_s[0]
            v2 = top_s[1]
            sc = _extract_top(_peer_candidates(v1, v2), PEER_TOPK)
            m_top = sc[0]
            z = jnp.zeros_like(m_top)
            for k in range(PEER_TOPK):
                z = z + jnp.exp(sc[k] - m_top)
            tau_s[h:h + 1, :] = sc[PEER_TOPK - 1]
            e1_s[h] = jnp.exp(s1_s[h] - tops[0]) / z
            e2_s[h] = jnp.exp(s2_s[h] - tops[1])
        acc_s[...] = jnp.zeros_like(acc_s)

    at_s[...] = jnp.dot(u_ref[...], ht_s[...], preferred_element_type=jnp.float32)

    def key_row(r, carry):
        e1 = j * rows_per_chunk + r
        base = pl.multiple_of(r * N_KEYS, N_KEYS)
        for g in range(N_KEYS // BF16_ROWS):
            r0 = g * BF16_ROWS
            gate = jnp.zeros((BF16_ROWS, tb), jnp.float32)
            for h in range(PEER_HEADS):
                s1row = s1_s[h, pl.ds(e1, 1), :]
                e1row = e1_s[h, pl.ds(e1, 1), :]
                ssum = s1row + s2_s[h, r0:r0 + BF16_ROWS, :]
                sel = ssum >= tau_s[h:h + 1, :]
                gate = gate + jnp.where(sel, e1row * e2_s[h, r0:r0 + BF16_ROWS, :], 0.0)
            a = at_s[pl.ds(base + r0, BF16_ROWS), :]
            act = 0.5 * a * (1.0 + lax.erf(a * SQRT_HALF))
            coef_s[pl.ds(base + r0, BF16_ROWS), :] = (gate * act).astype(jnp.bfloat16)
        return carry

    lax.fori_loop(0, rows_per_chunk, key_row, 0)

    acc_s[...] += jnp.dot(vt_ref[...], coef_s[...], preferred_element_type=jnp.float32)

    @pl.when(j == n_chunks - 1)
    def _epilogue():
        o_ref[...] = x_ref[...] + acc_s[...].T


def _peer_residual(x, g, w_q, sub_keys, u_tab, v_tab, *, tokens=PEER_TOKENS, chunk=PEER_EXPERT_CHUNK):
    n, d = x.shape
    assert n % tokens == 0 and N_EXPERTS % chunk == 0 and chunk % N_KEYS == 0
    wqt = w_q.T.astype(jnp.bfloat16)
    keys = sub_keys.astype(jnp.bfloat16)
    u_b = u_tab.astype(jnp.bfloat16)
    vt_b = v_tab.T.astype(jnp.bfloat16)
    f32 = jnp.float32
    per_head = pltpu.VMEM((PEER_HEADS, N_KEYS, tokens), f32)
    return pl.pallas_call(
        _peer_kernel,
        grid=(n // tokens, N_EXPERTS // chunk),
        in_specs=[
            pl.BlockSpec((tokens, d), lambda i, j: (i, 0)),
            pl.BlockSpec((1, d), lambda i, j: (0, 0)),
            pl.BlockSpec((PEER_HEADS * PEER_DKEY, d), lambda i, j: (0, 0)),
            pl.BlockSpec((PEER_HEADS, 2, N_KEYS, PEER_HALF), lambda i, j: (0, 0, 0, 0)),
            pl.BlockSpec((chunk, d), lambda i, j: (j, 0)),
            pl.BlockSpec((d, chunk), lambda i, j: (0, j)),
        ],
        out_specs=pl.BlockSpec((tokens, d), lambda i, j: (i, 0)),
        out_shape=jax.ShapeDtypeStruct((n, d), f32),
        scratch_shapes=[
            pltpu.VMEM((d, tokens), jnp.bfloat16),
            per_head, per_head, per_head, per_head,
            pltpu.VMEM((PEER_HEADS, tokens), f32),
            pltpu.VMEM((2, PEER_TOPK, tokens), f32),
            pltpu.VMEM((chunk, tokens), f32),
            pltpu.VMEM((chunk, tokens), jnp.bfloat16),
            pltpu.VMEM((d, tokens), f32),
        ],
        compiler_params=pltpu.CompilerParams(
            dimension_semantics=("arbitrary", "arbitrary"),
            vmem_limit_bytes=PEER_VMEM_LIMIT),
        name="peer",
    )(x, g.reshape(1, d), wqt, keys, u_b, vt_b)


def _rmsnorm(x, g):
    xf = x.astype(jnp.float32)
    y = xf * lax.rsqrt(jnp.mean(xf * xf, axis=-1, keepdims=True) + NORM_EPS)
    return (y * g.astype(jnp.float32)).astype(x.dtype)


def _layernorm(x, g, b):
    xf = x.astype(jnp.float32)
    mu = jnp.mean(xf, axis=-1, keepdims=True)
    var = jnp.mean(jnp.square(xf - mu), axis=-1, keepdims=True)
    y = (xf - mu) * lax.rsqrt(var + NORM_EPS)
    return (y * g.astype(jnp.float32) + b.astype(jnp.float32)).astype(x.dtype)


def _rope(x, pos):
    half = QK_ROPE // 2
    inv = ROPE_THETA ** (-jnp.arange(half, dtype=jnp.float32) / half)
    ang = pos.astype(jnp.float32)[:, None] * inv[None, :]
    ang = ang.reshape((ang.shape[0],) + (1,) * (x.ndim - 3) + (half,))
    cos, sin = jnp.cos(ang), jnp.sin(ang)
    xf = x.astype(jnp.float32)
    x1, x2 = xf[..., :half], xf[..., half:]
    return jnp.concatenate([x1 * cos - x2 * sin, x1 * sin + x2 * cos], axis=-1).astype(x.dtype)


def _even_project(h, pos, w_in, q_norm, w_uq, kv_norm, b_i, b_f):
    B, S, _ = h.shape
    f32 = jnp.float32
    z = h @ w_in
    cq, ckv, mq, mk, mv, mi, mf, mo = jnp.split(z, IN_SPLITS, axis=-1)
    q = (_rmsnorm(cq, q_norm) @ w_uq).reshape(B, S, MLA_HEADS, QK_NOPE + QK_ROPE)
    q_nope = q[..., :QK_NOPE]
    q_rope = _rope(q[..., QK_NOPE:], pos)
    c_kv = _rmsnorm(ckv[..., :KV_LORA], kv_norm)
    k_rope = _rope(ckv[..., KV_LORA:], pos)
    mq = mq.reshape(B, S, ML_HEADS, ML_DQK).astype(f32)
    mk = mk.reshape(B, S, ML_HEADS, ML_DQK).astype(f32) * (ML_DQK ** -0.5)
    mv = mv.reshape(B, S, ML_HEADS, ML_DV).astype(f32)
    i_pre = mi.astype(f32) + b_i.astype(f32)
    f_pre = mf.astype(f32) + b_f.astype(f32)
    return q_nope, q_rope, c_kv, k_rope, mq, mk, mv, i_pre, f_pre, mo


def _mla_prompt(q_nope, q_rope, c_kv, k_rope, w_uk, w_uv):
    S = q_nope.shape[1]
    scale = (QK_NOPE + QK_ROPE) ** -0.5
    k_nope = jnp.einsum('bsc,chd->bshd', c_kv, w_uk)
    v = jnp.einsum('bsc,chd->bshd', c_kv, w_uv)
    outs = []
    for qb in range(S // Q_BLOCK):
        lo, hi = qb * Q_BLOCK, (qb + 1) * Q_BLOCK
        s = (jnp.einsum('bqhd,bkhd->bhqk', q_nope[:, lo:hi], k_nope[:, :hi])
             + jnp.einsum('bqhr,bkr->bhqk', q_rope[:, lo:hi], k_rope[:, :hi])).astype(jnp.float32) * scale
        mask = (lo + jnp.arange(Q_BLOCK))[:, None] >= jnp.arange(hi)[None, :]
        p = jax.nn.softmax(jnp.where(mask, s, -jnp.inf), axis=-1).astype(v.dtype)
        outs.append(jnp.einsum('bhqk,bkhd->bqhd', p, v[:, :hi]))
    return jnp.concatenate(outs, axis=1)


def _mla_sample(q_nope, q_rope, c_new, kr_new, c_past, kr_past, w_uk, w_uv):
    T = q_nope.shape[1]
    P = c_past.shape[1]
    scale = (QK_NOPE + QK_ROPE) ** -0.5
    q_lat = jnp.einsum('bthd,chd->bthc', q_nope, w_uk)
    s_past = (jnp.einsum('bthc,bkc->bhtk', q_lat, c_past)
              + jnp.einsum('bthr,bkr->bhtk', q_rope, kr_past)).astype(jnp.float32) * scale
    s_new = (jnp.einsum('bthc,bkc->bhtk', q_lat, c_new)
             + jnp.einsum('bthr,bkr->bhtk', q_rope, kr_new)).astype(jnp.float32) * scale
    causal = jnp.tril(jnp.ones((T, T), dtype=bool))
    s_new = jnp.where(causal, s_new, -jnp.inf)
    p = jax.nn.softmax(jnp.concatenate([s_past, s_new], axis=-1), axis=-1).astype(c_new.dtype)
    o_lat = (jnp.einsum('bhtk,bkc->bthc', p[..., :P], c_past)
             + jnp.einsum('bhtk,bkc->bthc', p[..., P:], c_new))
    return jnp.einsum('bthc,chd->bthd', o_lat, w_uv)


def _mlstm_chunked(q, k, v, i_pre, f_pre, C0, n0, m0):
    B, S, H, _ = q.shape
    L = ML_CHUNK if S % ML_CHUNK == 0 else S
    nc = S // L
    logf = jax.nn.log_sigmoid(f_pre)

    def chunks(a):
        return jnp.moveaxis(a.reshape((B, nc, L) + a.shape[2:]), 1, 0)

    causal = jnp.tril(jnp.ones((L, L), dtype=bool))[None, :, :, None]

    def step(carry, inp):
        C, n, m = carry
        qc, kc, vc, ic, lfc = inp
        b = jnp.cumsum(lfc, axis=1)
        a = ic - b
        m_t = b + jnp.maximum(m[:, None, :], lax.cummax(a, axis=1))
        d_log = (b - m_t)[:, :, None, :] + a[:, None, :, :]
        w = jnp.exp(jnp.where(causal, d_log, -jnp.inf))
        inter = jnp.exp(m[:, None, :] + b - m_t)
        wqk = w * jnp.einsum('bthd,bshd->btsh', qc, kc)
        num = (jnp.einsum('btsh,bshv->bthv', wqk, vc)
               + inter[..., None] * jnp.einsum('bthd,bhdv->bthv', qc, C))
        den = wqk.sum(axis=2) + inter * jnp.einsum('bthd,bhd->bth', qc, n)
        h = num / jnp.maximum(jnp.abs(den), jnp.exp(-m_t))[..., None]
        m_last = m_t[:, -1]
        w_s = jnp.exp(a + b[:, -1:, :] - m_last[:, None, :])
        f_last = jnp.exp(m + b[:, -1] - m_last)
        C_new = f_last[..., None, None] * C + jnp.einsum('bsh,bshd,bshv->bhdv', w_s, kc, vc)
        n_new = f_last[..., None] * n + jnp.einsum('bsh,bshd->bhd', w_s, kc)
        return (C_new, n_new, m_last), h

    (C_f, n_f, m_f), hs = lax.scan(step, (C0, n0, m0),
                                   (chunks(q), chunks(k), chunks(v), chunks(i_pre), chunks(logf)))
    h = jnp.moveaxis(hs, 0, 1).reshape(B, S, H, v.shape[-1])
    return h, C_f, n_f, m_f


def _even_merge(att, h_ml, o_pre, ml_norm, w_out):
    B, S = att.shape[0], att.shape[1]
    hn = h_ml * lax.rsqrt(jnp.mean(h_ml * h_ml, axis=-1, keepdims=True) + NORM_EPS)
    y_ml = (hn.reshape(B, S, ML_HEADS * ML_DV) * ml_norm.astype(jnp.float32)
            * jax.nn.sigmoid(o_pre.astype(jnp.float32))).astype(att.dtype)
    mix = jnp.concatenate([att.reshape(B, S, MLA_HEADS * V_HEAD), y_ml], axis=-1)
    return mix @ w_out


def _conformer_conv(h, buf, w_pw1, b_pw1, w_dw, b_dw, ln_g, ln_b, w_pw2, b_pw2):
    a = h @ w_pw1 + b_pw1
    g = a[..., :D_MODEL] * jax.nn.sigmoid(a[..., D_MODEL:])
    full = jnp.concatenate([buf.astype(g.dtype), g], axis=1)
    y = lax.conv_general_dilated(full, w_dw[:, None, :].astype(full.dtype), window_strides=(1,),
                                 padding='VALID', dimension_numbers=('NWC', 'WIO', 'NWC'),
                                 feature_group_count=D_MODEL) + b_dw
    y = jax.nn.silu(_layernorm(y, ln_g, ln_b))
    return y @ w_pw2 + b_pw2, full[:, -(CONV_W - 1):]


def kernel(x_prompt, x_sample, cache_mla_latent, cache_mla_rope, state_mlstm_C, state_mlstm_n, state_mlstm_m, state_conv, page_table, attn_norm, w_in, q_norm, w_uq, kv_norm, w_uk, w_uv, b_igate, b_fgate, ml_norm, w_out, conv_norm, w_pw1, b_pw1, w_dw, b_dw, conv_ln_g, conv_ln_b, w_pw2, b_pw2, ffn_norm, peer_wq, peer_keys, peer_u, peer_v, final_norm):
    f32 = jnp.float32
    bp, sp = x_prompt.shape[0], x_prompt.shape[1]
    bs, ss = x_sample.shape[0], x_sample.shape[1]
    n_p, n_s = bp * sp, bs * ss
    pos_p = jnp.arange(sp, dtype=jnp.int32)
    pos_s = PAST_LEN + jnp.arange(ss, dtype=jnp.int32)
    past = page_table.shape[1] * PAGE_SIZE
    xp, xs = x_prompt, x_sample
    lat_p, rope_p, lat_s, rope_s = [], [], [], []
    cp_l, np_l, mp_l, cs_l, ns_l, ms_l = [], [], [], [], [], []
    convp_l, convs_l = [], []
    for layer in range(DEPTH):
        j = layer // 2
        if layer % 2 == 0:
            (qn_p, qr_p, c_p, kr_p, mq_p, mk_p, mv_p, ig_p, fg_p, og_p) = _even_project(
                _rmsnorm(xp, attn_norm[j]), pos_p, w_in[j], q_norm[j], w_uq[j], kv_norm[j], b_igate[j], b_fgate[j])
            (qn_s, qr_s, c_s, kr_s, mq_s, mk_s, mv_s, ig_s, fg_s, og_s) = _even_project(
                _rmsnorm(xs, attn_norm[j]), pos_s, w_in[j], q_norm[j], w_uq[j], kv_norm[j], b_igate[j], b_fgate[j])
            att_p = _mla_prompt(qn_p, qr_p, c_p, kr_p, w_uk[j], w_uv[j])
            c_past = cache_mla_latent[j][page_table].reshape(bs, past, KV_LORA)
            kr_past = cache_mla_rope[j][page_table].reshape(bs, past, QK_ROPE)
            att_s = _mla_sample(qn_s, qr_s, c_s, kr_s, c_past, kr_past, w_uk[j], w_uv[j])
            C0 = jnp.zeros((bp, ML_HEADS, ML_DQK, ML_DV), f32)
            n0 = jnp.zeros((bp, ML_HEADS, ML_DQK), f32)
            m0 = jnp.full((bp, ML_HEADS), M_INIT, f32)
            h_p, C_p, n_p_, m_p = _mlstm_chunked(mq_p, mk_p, mv_p, ig_p, fg_p, C0, n0, m0)
            h_s, C_s, n_s_, m_s = _mlstm_chunked(mq_s, mk_s, mv_s, ig_s, fg_s,
                                                 state_mlstm_C[j].astype(f32), state_mlstm_n[j].astype(f32),
                                                 state_mlstm_m[j].astype(f32))
            xp = xp + _even_merge(att_p, h_p, og_p, ml_norm[j], w_out[j])
            xs = xs + _even_merge(att_s, h_s, og_s, ml_norm[j], w_out[j])
            lat_p.append(c_p)
            rope_p.append(kr_p)
            lat_s.append(c_s)
            rope_s.append(kr_s)
            cp_l.append(C_p)
            np_l.append(n_p_)
            mp_l.append(m_p)
            cs_l.append(C_s)
            ns_l.append(n_s_)
            ms_l.append(m_s)
        else:
            buf0 = jnp.zeros((bp, CONV_W - 1, D_MODEL), xp.dtype)
            out_p, buf_p = _conformer_conv(_rmsnorm(xp, conv_norm[j]), buf0, w_pw1[j], b_pw1[j], w_dw[j], b_dw[j],
                                           conv_ln_g[j], conv_ln_b[j], w_pw2[j], b_pw2[j])
            out_s, buf_s = _conformer_conv(_rmsnorm(xs, conv_norm[j]), state_conv[j], w_pw1[j], b_pw1[j], w_dw[j],
                                           b_dw[j], conv_ln_g[j], conv_ln_b[j], w_pw2[j], b_pw2[j])
            xp = xp + out_p
            xs = xs + out_s
            convp_l.append(buf_p)
            convs_l.append(buf_s)
        rows = jnp.concatenate([xp.reshape(n_p, D_MODEL), xs.reshape(n_s, D_MODEL)], axis=0)
        rows = _peer_residual(rows, ffn_norm[layer], peer_wq[layer], peer_keys[layer],
                              peer_u[layer], peer_v[layer])
        xp = rows[:n_p].reshape(bp, sp, D_MODEL)
        xs = rows[n_p:].reshape(bs, ss, D_MODEL)
    y_prompt = _rmsnorm(xp, final_norm)
    y_sample = _rmsnorm(xs, final_norm)
    return (y_prompt, y_sample, jnp.stack(lat_p), jnp.stack(rope_p), jnp.stack(lat_s), jnp.stack(rope_s),
            jnp.stack(cp_l), jnp.stack(np_l), jnp.stack(mp_l), jnp.stack(cs_l), jnp.stack(ns_l), jnp.stack(ms_l),
            jnp.stack(convp_l), jnp.stack(convs_l))
```

```python
import functools
import math

import jax
import jax.numpy as jnp
import numpy as np
from jax import lax
from jax.experimental import pallas as pl
from jax.experimental.pallas import tpu as pltpu

D_MODEL = 1024
DEPTH = 2
PAST_LEN = 8192
PAGE_SIZE = 128
NORM_EPS = 1e-6

MLA_HEADS = 8
Q_LORA = 384
KV_LORA = 256
QK_NOPE = 64
QK_ROPE = 32
V_HEAD = 64
ROPE_THETA = 10000.0
Q_BLOCK = 128

ML_HEADS = 4
ML_DQK = 128
ML_DV = 128
ML_CHUNK = 128
M_INIT = -1e30

IN_WIDTHS = (Q_LORA, KV_LORA + QK_ROPE, ML_HEADS * ML_DQK, ML_HEADS * ML_DQK,
             ML_HEADS * ML_DV, ML_HEADS, ML_HEADS, ML_HEADS * ML_DV)
IN_SPLITS = tuple(int(c) for c in np.cumsum(IN_WIDTHS)[:-1])

CONV_W = 31

PEER_HEADS = 8
N_KEYS = 128
N_EXPERTS = N_KEYS * N_KEYS
PEER_DKEY = 256
PEER_HALF = PEER_DKEY // 2
PEER_TOPK = 16

SUBLANES = 8
BF16_ROWS = 16
PEER_TOKENS = 512
PEER_EXPERT_CHUNK = 1024
PEER_VMEM_LIMIT = 56 * 1024 * 1024
FLASH_BLOCK = 512
FLASH_VMEM_LIMIT = 32 * 1024 * 1024

NEG_INF = float("-inf")
SQRT_HALF = math.sqrt(0.5)


def _peer_candidates(v1, v2):
    t = v1.shape[1]
    row = lax.broadcasted_iota(jnp.int32, (8, t), 0)
    slabs = [v1[0:8] + v2[0:1], v1[8:16] + v2[0:1], v1[0:8] + v2[1:2]]
    for b in range(2, 8):
        n_a = PEER_TOPK // (b + 1)
        slabs.append(jnp.where(row < n_a, v1[0:8] + v2[b:b + 1], NEG_INF))
    slabs.append(v1[0:1] + v2[8:16])
    return jnp.concatenate(slabs, axis=0)


def _extract_top(cur, n):
    vals = []
    for i in range(n):
        m = jnp.max(cur, axis=0, keepdims=True)
        vals.append(m)
        if i + 1 < n:
            cur = jnp.where(cur == m, NEG_INF, cur)
    return vals


def _peer_prologue(x_ref, g_ref, wqt_ref, keys_ref, ht_s, s1_s, s2_s, e1_s, e2_s, tau_s, top_s):
    tb = x_ref.shape[0]
    x = x_ref[...]
    ms = jnp.mean(x * x, axis=-1, keepdims=True)
    hn = (x * lax.rsqrt(ms + NORM_EPS)) * g_ref[...]
    ht = hn.T.astype(jnp.bfloat16)
    ht_s[...] = ht
    qtb = jnp.dot(wqt_ref[...], ht, preferred_element_type=jnp.float32).astype(jnp.bfloat16)
    for h in range(PEER_HEADS):
        tops = []
        for p, s_ref in enumerate((s1_s, s2_s)):
            r0 = (h * 2 + p) * PEER_HALF
            s = jnp.dot(keys_ref[h, p], qtb[r0:r0 + PEER_HALF, :],
                        preferred_element_type=jnp.float32)
            s_ref[h] = s
            vals = _extract_top(s, PEER_TOPK)
            for i, m in enumerate(vals):
                top_s[p, i:i + 1, :] = m
            tops.append(vals[0])
        sc = _extract_top(_peer_candidates(top_s[0], top_s[1]), PEER_TOPK)
        z = jnp.zeros_like(sc[0])
        for k in range(PEER_TOPK):
            z = z + jnp.exp(sc[k] - sc[0])
        tau_s[h] = jnp.broadcast_to(sc[PEER_TOPK - 1], (SUBLANES, tb))
        e1_s[h] = jnp.exp(s1_s[h] - tops[0]) / z
        e2_s[h] = jnp.exp(s2_s[h] - tops[1])


def _peer_gate_chunk(j, s1_s, s2_s, e1_s, e2_s, tau_s, rep_s, at_ref, coef_ref):
    rows = at_ref.shape[0] // N_KEYS
    tb = at_ref.shape[1]
    e1_0 = pl.multiple_of(j * rows, SUBLANES)
    for h in range(PEER_HEADS):
        s1_blk = s1_s[h, pl.ds(e1_0, rows), :]
        e1_blk = e1_s[h, pl.ds(e1_0, rows), :]
        for r in range(rows):
            rep_s[0, h, r] = jnp.broadcast_to(s1_blk[r:r + 1, :], (SUBLANES, tb))
            rep_s[1, h, r] = jnp.broadcast_to(e1_blk[r:r + 1, :], (SUBLANES, tb))
    for r in range(rows):
        for g in range(N_KEYS // BF16_ROWS):
            halves = []
            for half in range(BF16_ROWS // SUBLANES):
                r0 = g * BF16_ROWS + half * SUBLANES
                gate = None
                for h in range(PEER_HEADS):
                    ssum = rep_s[0, h, r] + s2_s[h, r0:r0 + SUBLANES, :]
                    prod = rep_s[1, h, r] * e2_s[h, r0:r0 + SUBLANES, :]
                    term = jnp.where(ssum >= tau_s[h], prod, 0.0)
                    gate = term if gate is None else gate + term
                a = at_ref[r * N_KEYS + r0:r * N_KEYS + r0 + SUBLANES, :]
                halves.append(gate * ((0.5 * a) * (1.0 + lax.erf(a * SQRT_HALF))))
            row0 = r * N_KEYS + g * BF16_ROWS
            coef_ref[row0:row0 + BF16_ROWS, :] = jnp.concatenate(halves, axis=0).astype(jnp.bfloat16)


def _peer_kernel(x_ref, g_ref, wqt_ref, keys_ref, u0_ref, u_ref, vt_ref, vtl_ref, o_ref,
                 ht_s, s1_s, s2_s, e1_s, e2_s, tau_s, top_s, rep_s,
                 at_a, at_b, coef_a, coef_b, acc_s):
    j = pl.program_id(1)
    n_chunks = pl.num_programs(1)

    @pl.when(j == 0)
    def _first():
        _peer_prologue(x_ref, g_ref, wqt_ref, keys_ref, ht_s, s1_s, s2_s, e1_s, e2_s, tau_s, top_s)
        at_a[...] = jnp.dot(u0_ref[...], ht_s[...], preferred_element_type=jnp.float32)
        coef_b[...] = jnp.zeros_like(coef_b)
        acc_s[...] = jnp.zeros_like(acc_s)

    def step(at_cur, at_nxt, coef_cur, coef_prv):
        at_nxt[...] = jnp.dot(u_ref[...], ht_s[...], preferred_element_type=jnp.float32)
        acc_s[...] += jnp.dot(vt_ref[...], coef_prv[...], preferred_element_type=jnp.float32)
        _peer_gate_chunk(j, s1_s, s2_s, e1_s, e2_s, tau_s, rep_s, at_cur, coef_cur)

    @pl.when(j % 2 == 0)
    def _even():
        step(at_a, at_b, coef_a, coef_b)

    @pl.when(j % 2 == 1)
    def _odd():
        step(at_b, at_a, coef_b, coef_a)

    @pl.when(j == n_chunks - 1)
    def _last():
        acc = acc_s[...] + jnp.dot(vtl_ref[...], coef_b[...], preferred_element_type=jnp.float32)
        o_ref[...] = x_ref[...] + acc.T


def _peer_residual(x, g, w_q, sub_keys, u_tab, v_tab, *, tokens=PEER_TOKENS, chunk=PEER_EXPERT_CHUNK):
    n, d = x.shape
    n_chunks = N_EXPERTS // chunk
    assert n % tokens == 0 and N_EXPERTS % chunk == 0 and chunk == SUBLANES * N_KEYS and n_chunks % 2 == 0
    wqt = w_q.T.astype(jnp.bfloat16)
    keys = sub_keys.astype(jnp.bfloat16)
    u_b = u_tab.astype(jnp.bfloat16)
    vt_b = v_tab.T.astype(jnp.bfloat16)
    f32 = jnp.float32
    once = pl.Buffered(1)
    per_head = pltpu.VMEM((PEER_HEADS, N_KEYS, tokens), f32)
    at_buf = pltpu.VMEM((chunk, tokens), f32)
    coef_buf = pltpu.VMEM((chunk, tokens), jnp.bfloat16)
    return pl.pallas_call(
        _peer_kernel,
        grid=(n // tokens, n_chunks),
        in_specs=[
            pl.BlockSpec((tokens, d), lambda i, j: (i, 0)),
            pl.BlockSpec((1, d), lambda i, j: (0, 0), pipeline_mode=once),
            pl.BlockSpec((PEER_HEADS * PEER_DKEY, d), lambda i, j: (0, 0), pipeline_mode=once),
            pl.BlockSpec((PEER_HEADS, 2, N_KEYS, PEER_HALF), lambda i, j: (0, 0, 0, 0), pipeline_mode=once),
            pl.BlockSpec((chunk, d), lambda i, j: (0, 0), pipeline_mode=once),
            pl.BlockSpec((chunk, d), lambda i, j: (jnp.minimum(j + 1, n_chunks - 1), 0)),
            pl.BlockSpec((d, chunk), lambda i, j: (0, jnp.maximum(j - 1, 0))),
            pl.BlockSpec((d, chunk), lambda i, j: (0, n_chunks - 1), pipeline_mode=once),
        ],
        out_specs=pl.BlockSpec((tokens, d), lambda i, j: (i, 0)),
        out_shape=jax.ShapeDtypeStruct((n, d), f32),
        scratch_shapes=[
            pltpu.VMEM((d, tokens), jnp.bfloat16),
            per_head, per_head, per_head, per_head,
            pltpu.VMEM((PEER_HEADS, SUBLANES, tokens), f32),
            pltpu.VMEM((2, PEER_TOPK, tokens), f32),
            pltpu.VMEM((2, PEER_HEADS, SUBLANES, SUBLANES, tokens), f32),
            at_buf, at_buf, coef_buf, coef_buf,
            pltpu.VMEM((d, tokens), f32),
        ],
        compiler_params=pltpu.CompilerParams(
            dimension_semantics=("arbitrary", "arbitrary"),
            vmem_limit_bytes=PEER_VMEM_LIMIT),
        name="peer",
    )(x, g.reshape(1, d), wqt, keys, u_b, u_b, vt_b, vt_b)


def _flash_kernel(q_ref, k_ref, v_ref, o_ref, *, scale, block):
    qi = pl.program_id(2)
    q = q_ref[0, 0]

    def kv_block(kb, carry, diagonal):
        m, l, acc = carry
        k0 = pl.multiple_of(kb * block, block)
        k = k_ref[0, 0, pl.ds(k0, block), :]
        v = v_ref[0, 0, pl.ds(k0, block), :]
        s = lax.dot_general(q, k, (((1,), (1,)), ((), ())), preferred_element_type=jnp.float32) * scale
        if diagonal:
            row = lax.broadcasted_iota(jnp.int32, s.shape, 0)
            col = lax.broadcasted_iota(jnp.int32, s.shape, 1)
            s = jnp.where(row >= col, s, NEG_INF)
        m_new = jnp.maximum(m, jnp.max(s, axis=1, keepdims=True))
        alpha = jnp.exp(m - m_new)
        p = jnp.exp(s - m_new)
        l_new = alpha * l + jnp.sum(p, axis=1, keepdims=True)
        acc_new = alpha * acc + jnp.dot(p.astype(v.dtype), v, preferred_element_type=jnp.float32)
        return m_new, l_new, acc_new

    init = (jnp.full((block, 1), NEG_INF, jnp.float32), jnp.zeros((block, 1), jnp.float32),
            jnp.zeros((block, v_ref.shape[3]), jnp.float32))
    carry = lax.fori_loop(0, qi, lambda kb, c: kv_block(kb, c, False), init)
    _, l, acc = kv_block(qi, carry, True)
    o_ref[0, 0] = acc / l


def _flash_attention(q, k, v, scale, block=FLASH_BLOCK):
    b, h, s, d_qk = q.shape
    d_v = v.shape[3]
    assert s % block == 0
    return pl.pallas_call(
        functools.partial(_flash_kernel, scale=scale, block=block),
        grid=(b, h, s // block),
        in_specs=[
            pl.BlockSpec((1, 1, block, d_qk), lambda bi, hi, qi: (bi, hi, qi, 0)),
            pl.BlockSpec((1, 1, s, d_qk), lambda bi, hi, qi: (bi, hi, 0, 0)),
            pl.BlockSpec((1, 1, s, d_v), lambda bi, hi, qi: (bi, hi, 0, 0)),
        ],
        out_specs=pl.BlockSpec((1, 1, block, d_v), lambda bi, hi, qi: (bi, hi, qi, 0)),
        out_shape=jax.ShapeDtypeStruct((b, h, s, d_v), jnp.float32),
        compiler_params=pltpu.CompilerParams(
            dimension_semantics=("arbitrary", "arbitrary", "arbitrary"),
            vmem_limit_bytes=FLASH_VMEM_LIMIT),
        name="mla_prompt_attention",
    )(q, k, v)


def _rmsnorm(x, g):
    xf = x.astype(jnp.float32)
    y = xf * lax.rsqrt(jnp.mean(xf * xf, axis=-1, keepdims=True) + NORM_EPS)
    return (y * g.astype(jnp.float32)).astype(x.dtype)


def _layernorm(x, g, b):
    xf = x.astype(jnp.float32)
    mu = jnp.mean(xf, axis=-1, keepdims=True)
    var = jnp.mean(jnp.square(xf - mu), axis=-1, keepdims=True)
    y = (xf - mu) * lax.rsqrt(var + NORM_EPS)
    return (y * g.astype(jnp.float32) + b.astype(jnp.float32)).astype(x.dtype)


def _rope(x, pos):
    half = QK_ROPE // 2
    inv = ROPE_THETA ** (-jnp.arange(half, dtype=jnp.float32) / half)
    ang = pos.astype(jnp.float32)[:, None] * inv[None, :]
    ang = ang.reshape((ang.shape[0],) + (1,) * (x.ndim - 3) + (half,))
    cos, sin = jnp.cos(ang), jnp.sin(ang)
    xf = x.astype(jnp.float32)
    x1, x2 = xf[..., :half], xf[..., half:]
    return jnp.concatenate([x1 * cos - x2 * sin, x1 * sin + x2 * cos], axis=-1).astype(x.dtype)


def _even_project(h, pos, w_in, q_norm, w_uq, kv_norm, b_i, b_f):
    B, S, _ = h.shape
    f32 = jnp.float32
    z = h @ w_in
    cq, ckv, mq, mk, mv, mi, mf, mo = jnp.split(z, IN_SPLITS, axis=-1)
    q = (_rmsnorm(cq, q_norm) @ w_uq).reshape(B, S, MLA_HEADS, QK_NOPE + QK_ROPE)
    q_nope = q[..., :QK_NOPE]
    q_rope = _rope(q[..., QK_NOPE:], pos)
    c_kv = _rmsnorm(ckv[..., :KV_LORA], kv_norm)
    k_rope = _rope(ckv[..., KV_LORA:], pos)
    mq = mq.reshape(B, S, ML_HEADS, ML_DQK).astype(f32)
    mk = mk.reshape(B, S, ML_HEADS, ML_DQK).astype(f32) * (ML_DQK ** -0.5)
    mv = mv.reshape(B, S, ML_HEADS, ML_DV).astype(f32)
    i_pre = mi.astype(f32) + b_i.astype(f32)
    f_pre = mf.astype(f32) + b_f.astype(f32)
    return q_nope, q_rope, c_kv, k_rope, mq, mk, mv, i_pre, f_pre, mo


def _mla_prompt(q_nope, q_rope, c_kv, k_rope, w_uk, w_uv):
    B, S = q_nope.shape[0], q_nope.shape[1]
    bf16 = jnp.bfloat16
    scale = (QK_NOPE + QK_ROPE) ** -0.5
    k_nope = jnp.einsum('bsc,chd->bshd', c_kv, w_uk)
    v = jnp.einsum('bsc,chd->bshd', c_kv, w_uv)
    q = jnp.concatenate([q_nope, q_rope], axis=-1)
    k = jnp.concatenate([k_nope, jnp.broadcast_to(k_rope[:, :, None, :], (B, S, MLA_HEADS, QK_ROPE))], axis=-1)
    heads_major = lambda a: jnp.transpose(a, (0, 2, 1, 3)).astype(bf16)
    o = _flash_attention(heads_major(q), heads_major(k), heads_major(v), scale)
    return jnp.transpose(o, (0, 2, 1, 3))


def _mla_sample(q_nope, q_rope, c_new, kr_new, c_past, kr_past, w_uk, w_uv):
    T = q_nope.shape[1]
    P = c_past.shape[1]
    scale = (QK_NOPE + QK_ROPE) ** -0.5
    q_lat = jnp.einsum('bthd,chd->bthc', q_nope, w_uk)
    s_past = (jnp.einsum('bthc,bkc->bhtk', q_lat, c_past)
              + jnp.einsum('bthr,bkr->bhtk', q_rope, kr_past)).astype(jnp.float32) * scale
    s_new = (jnp.einsum('bthc,bkc->bhtk', q_lat, c_new)
             + jnp.einsum('bthr,bkr->bhtk', q_rope, kr_new)).astype(jnp.float32) * scale
    causal = jnp.tril(jnp.ones((T, T), dtype=bool))
    s_new = jnp.where(causal, s_new, -jnp.inf)
    p = jax.nn.softmax(jnp.concatenate([s_past, s_new], axis=-1), axis=-1).astype(c_new.dtype)
    o_lat = (jnp.einsum('bhtk,bkc->bthc', p[..., :P], c_past)
             + jnp.einsum('bhtk,bkc->bthc', p[..., P:], c_new))
    return jnp.einsum('bthc,chd->bthd', o_lat, w_uv)


def _mlstm_chunked(q, k, v, i_pre, f_pre, C0, n0, m0):
    B, S, H, _ = q.shape
    L = ML_CHUNK if S % ML_CHUNK == 0 else S
    nc = S // L
    logf = jax.nn.log_sigmoid(f_pre)

    def chunks(a):
        return jnp.moveaxis(a.reshape((B, nc, L) + a.shape[2:]), 1, 0)

    causal = jnp.tril(jnp.ones((L, L), dtype=bool))[None, :, :, None]

    def step(carry, inp):
        C, n, m = carry
        qc, kc, vc, ic, lfc = inp
        b = jnp.cumsum(lfc, axis=1)
        a = ic - b
        m_t = b + jnp.maximum(m[:, None, :], lax.cummax(a, axis=1))
        d_log = (b - m_t)[:, :, None, :] + a[:, None, :, :]
        w = jnp.exp(jnp.where(causal, d_log, -jnp.inf))
        inter = jnp.exp(m[:, None, :] + b - m_t)
        wqk = w * jnp.einsum('bthd,bshd->btsh', qc, kc)
        num = (jnp.einsum('btsh,bshv->bthv', wqk, vc)
               + inter[..., None] * jnp.einsum('bthd,bhdv->bthv', qc, C))
        den = wqk.sum(axis=2) + inter * jnp.einsum('bthd,bhd->bth', qc, n)
        h = num / jnp.maximum(jnp.abs(den), jnp.exp(-m_t))[..., None]
        m_last = m_t[:, -1]
        w_s = jnp.exp(a + b[:, -1:, :] - m_last[:, None, :])
        f_last = jnp.exp(m + b[:, -1] - m_last)
        C_new = f_last[..., None, None] * C + jnp.einsum('bsh,bshd,bshv->bhdv', w_s, kc, vc)
        n_new = f_last[..., None] * n + jnp.einsum('bsh,bshd->bhd', w_s, kc)
        return (C_new, n_new, m_last), h

    (C_f, n_f, m_f), hs = lax.scan(step, (C0, n0, m0),
                                   (chunks(q), chunks(k), chunks(v), chunks(i_pre), chunks(logf)))
    h = jnp.moveaxis(hs, 0, 1).reshape(B, S, H, v.shape[-1])
    return h, C_f, n_f, m_f


def _even_merge(att, h_ml, o_pre, ml_norm, w_out):
    B, S = att.shape[0], att.shape[1]
    hn = h_ml * lax.rsqrt(jnp.mean(h_ml * h_ml, axis=-1, keepdims=True) + NORM_EPS)
    y_ml = (hn.reshape(B, S, ML_HEADS * ML_DV) * ml_norm.astype(jnp.float32)
            * jax.nn.sigmoid(o_pre.astype(jnp.float32))).astype(att.dtype)
    mix = jnp.concatenate([att.reshape(B, S, MLA_HEADS * V_HEAD), y_ml], axis=-1)
    return mix @ w_out


def _conformer_conv(h, buf, w_pw1, b_pw1, w_dw, b_dw, ln_g, ln_b, w_pw2, b_pw2):
    a = h @ w_pw1 + b_pw1
    g = a[..., :D_MODEL] * jax.nn.sigmoid(a[..., D_MODEL:])
    full = jnp.concatenate([buf.astype(g.dtype), g], axis=1)
    y = lax.conv_general_dilated(full, w_dw[:, None, :].astype(full.dtype), window_strides=(1,),
                                 padding='VALID', dimension_numbers=('NWC', 'WIO', 'NWC'),
                                 feature_group_count=D_MODEL) + b_dw
    y = jax.nn.silu(_layernorm(y, ln_g, ln_b))
    return y @ w_pw2 + b_pw2, full[:, -(CONV_W - 1):]


def kernel(x_prompt, x_sample, cache_mla_latent, cache_mla_rope, state_mlstm_C, state_mlstm_n, state_mlstm_m, state_conv, page_table, attn_norm, w_in, q_norm, w_uq, kv_norm, w_uk, w_uv, b_igate, b_fgate, ml_norm, w_out, conv_norm, w_pw1, b_pw1, w_dw, b_dw, conv_ln_g, conv_ln_b, w_pw2, b_pw2, ffn_norm, peer_wq, peer_keys, peer_u, peer_v, final_norm):
    f32 = jnp.float32
    bp, sp = x_prompt.shape[0], x_prompt.shape[1]
    bs, ss = x_sample.shape[0], x_sample.shape[1]
    n_p, n_s = bp * sp, bs * ss
    pos_p = jnp.arange(sp, dtype=jnp.int32)
    pos_s = PAST_LEN + jnp.arange(ss, dtype=jnp.int32)
    past = page_table.shape[1] * PAGE_SIZE
    xp, xs = x_prompt, x_sample
    lat_p, rope_p, lat_s, rope_s = [], [], [], []
    cp_l, np_l, mp_l, cs_l, ns_l, ms_l = [], [], [], [], [], []
    convp_l, convs_l = [], []
    for layer in range(DEPTH):
        j = layer // 2
        if layer % 2 == 0:
            (qn_p, qr_p, c_p, kr_p, mq_p, mk_p, mv_p, ig_p, fg_p, og_p) = _even_project(
                _rmsnorm(xp, attn_norm[j]), pos_p, w_in[j], q_norm[j], w_uq[j], kv_norm[j], b_igate[j], b_fgate[j])
            (qn_s, qr_s, c_s, kr_s, mq_s, mk_s, mv_s, ig_s, fg_s, og_s) = _even_project(
                _rmsnorm(xs, attn_norm[j]), pos_s, w_in[j], q_norm[j], w_uq[j], kv_norm[j], b_igate[j], b_fgate[j])
            att_p = _mla_prompt(qn_p, qr_p, c_p, kr_p, w_uk[j], w_uv[j])
            c_past = cache_mla_latent[j][page_table].reshape(bs, past, KV_LORA)
            kr_past = cache_mla_rope[j][page_table].reshape(bs, past, QK_ROPE)
            att_s = _mla_sample(qn_s, qr_s, c_s, kr_s, c_past, kr_past, w_uk[j], w_uv[j])
            C0 = jnp.zeros((bp, ML_HEADS, ML_DQK, ML_DV), f32)
            n0 = jnp.zeros((bp, ML_HEADS, ML_DQK), f32)
            m0 = jnp.full((bp, ML_HEADS), M_INIT, f32)
            h_p, C_p, n_p_, m_p = _mlstm_chunked(mq_p, mk_p, mv_p, ig_p, fg_p, C0, n0, m0)
            h_s, C_s, n_s_, m_s = _mlstm_chunked(mq_s, mk_s, mv_s, ig_s, fg_s,
                                                 state_mlstm_C[j].astype(f32), state_mlstm_n[j].astype(f32),
                                                 state_mlstm_m[j].astype(f32))
            xp = xp + _even_merge(att_p, h_p, og_p, ml_norm[j], w_out[j])
            xs = xs + _even_merge(att_s, h_s, og_s, ml_norm[j], w_out[j])
            lat_p.append(c_p)
            rope_p.append(kr_p)
            lat_s.append(c_s)
            rope_s.append(kr_s)
            cp_l.append(C_p)
            np_l.append(n_p_)
            mp_l.append(m_p)
            cs_l.append(C_s)
            ns_l.append(n_s_)
            ms_l.append(m_s)
        else:
            buf0 = jnp.zeros((bp, CONV_W - 1, D_MODEL), xp.dtype)
            out_p, buf_p = _conformer_conv(_rmsnorm(xp, conv_norm[j]), buf0, w_pw1[j], b_pw1[j], w_dw[j], b_dw[j],
                                           conv_ln_g[j], conv_ln_b[j], w_pw2[j], b_pw2[j])
            out_s, buf_s = _conformer_conv(_rmsnorm(xs, conv_norm[j]), state_conv[j], w_pw1[j], b_pw1[j], w_dw[j],
                                           b_dw[j], conv_ln_g[j], conv_ln_b[j], w_pw2[j], b_pw2[j])
            xp = xp + out_p
            xs = xs + out_s
            convp_l.append(buf_p)
            convs_l.append(buf_s)
        rows = jnp.concatenate([xp.reshape(n_p, D_MODEL), xs.reshape(n_s, D_MODEL)], axis=0)
        rows = _peer_residual(rows, ffn_norm[layer], peer_wq[layer], peer_keys[layer],
                              peer_u[layer], peer_v[layer])
        xp = rows[:n_p].reshape(bp, sp, D_MODEL)
        xs = rows[n_p:].reshape(bs, ss, D_MODEL)
    y_prompt = _rmsnorm(xp, final_norm)
    y_sample = _rmsnorm(xs, final_norm)
    return (y_prompt, y_sample, jnp.stack(lat_p), jnp.stack(rope_p), jnp.stack(lat_s), jnp.stack(rope_s),
            jnp.stack(cp_l), jnp.stack(np_l), jnp.stack(mp_l), jnp.stack(cs_l), jnp.stack(ns_l), jnp.stack(ms_l),
            jnp.stack(convp_l), jnp.stack(convs_l))
```

```python
import functools
import math

import jax
import jax.numpy as jnp
import numpy as np
from jax import lax
from jax.experimental import pallas as pl
from jax.experimental.pallas import tpu as pltpu

D_MODEL = 1024
DEPTH = 2
PAST_LEN = 8192
PAGE_SIZE = 128
NORM_EPS = 1e-6

MLA_HEADS = 8
Q_LORA = 384
KV_LORA = 256
QK_NOPE = 64
QK_ROPE = 32
V_HEAD = 64
ROPE_THETA = 10000.0
Q_BLOCK = 128

ML_HEADS = 4
ML_DQK = 128
ML_DV = 128
ML_CHUNK = 128
M_INIT = -1e30

IN_WIDTHS = (Q_LORA, KV_LORA + QK_ROPE, ML_HEADS * ML_DQK, ML_HEADS * ML_DQK,
             ML_HEADS * ML_DV, ML_HEADS, ML_HEADS, ML_HEADS * ML_DV)
IN_SPLITS = tuple(int(c) for c in np.cumsum(IN_WIDTHS)[:-1])

CONV_W = 31

PEER_HEADS = 8
N_KEYS = 128
N_EXPERTS = N_KEYS * N_KEYS
PEER_DKEY = 256
PEER_HALF = PEER_DKEY // 2
PEER_TOPK = 16

SUBLANES = 8
BF16_ROWS = 16
PEER_TOKENS = 512
PEER_EXPERT_CHUNK = 1024
PEER_VMEM_LIMIT = 56 * 1024 * 1024
FLASH_BLOCK = 512
DECODE_PAGES = 32
DEC_PAD = 16
FLASH_VMEM_LIMIT = 32 * 1024 * 1024

NEG_INF = float("-inf")
SQRT_HALF = math.sqrt(0.5)


def _peer_candidates(v1, v2):
    t = v1.shape[1]
    row = lax.broadcasted_iota(jnp.int32, (8, t), 0)
    slabs = [v1[0:8] + v2[0:1], v1[8:16] + v2[0:1], v1[0:8] + v2[1:2]]
    for b in range(2, 8):
        n_a = PEER_TOPK // (b + 1)
        slabs.append(jnp.where(row < n_a, v1[0:8] + v2[b:b + 1], NEG_INF))
    slabs.append(v1[0:1] + v2[8:16])
    return jnp.concatenate(slabs, axis=0)


def _extract_top(cur, n):
    vals = []
    for i in range(n):
        m = jnp.max(cur, axis=0, keepdims=True)
        vals.append(m)
        if i + 1 < n:
            cur = jnp.where(cur == m, NEG_INF, cur)
    return vals


def _peer_prologue(x_ref, g_ref, wqt_ref, keys_ref, ht_s, s1_s, s2_s, e1_s, e2_s, tau_s, top_s):
    tb = x_ref.shape[0]
    x = x_ref[...]
    ms = jnp.mean(x * x, axis=-1, keepdims=True)
    hn = (x * lax.rsqrt(ms + NORM_EPS)) * g_ref[...]
    ht = hn.T.astype(jnp.bfloat16)
    ht_s[...] = ht
    qtb = jnp.dot(wqt_ref[...], ht, preferred_element_type=jnp.float32).astype(jnp.bfloat16)
    for h in range(PEER_HEADS):
        tops = []
        for p, s_ref in enumerate((s1_s, s2_s)):
            r0 = (h * 2 + p) * PEER_HALF
            s = jnp.dot(keys_ref[h, p], qtb[r0:r0 + PEER_HALF, :],
                        preferred_element_type=jnp.float32)
            s_ref[h] = s
            vals = _extract_top(s, PEER_TOPK)
            for i, m in enumerate(vals):
                top_s[p, i:i + 1, :] = m
            tops.append(vals[0])
        sc = _extract_top(_peer_candidates(top_s[0], top_s[1]), PEER_TOPK)
        z = jnp.zeros_like(sc[0])
        for k in range(PEER_TOPK):
            z = z + jnp.exp(sc[k] - sc[0])
        tau_s[h] = jnp.broadcast_to(sc[PEER_TOPK - 1], (SUBLANES, tb))
        e1_s[h] = jnp.exp(s1_s[h] - tops[0]) / z
        e2_s[h] = jnp.exp(s2_s[h] - tops[1])


def _peer_gate_chunk(j, s1_s, s2_s, e1_s, e2_s, tau_s, rep_s, at_ref, coef_ref, between=None):
    rows = at_ref.shape[0] // N_KEYS
    tb = at_ref.shape[1]
    e1_0 = pl.multiple_of(j * rows, SUBLANES)
    for h in range(PEER_HEADS):
        s1_blk = s1_s[h, pl.ds(e1_0, rows), :]
        e1_blk = e1_s[h, pl.ds(e1_0, rows), :]
        for r in range(rows):
            rep_s[0, h, r] = jnp.broadcast_to(s1_blk[r:r + 1, :], (SUBLANES, tb))
            rep_s[1, h, r] = jnp.broadcast_to(e1_blk[r:r + 1, :], (SUBLANES, tb))
    for r in range(rows):
        if between is not None:
            between(r)
        for g in range(N_KEYS // BF16_ROWS):
            halves = []
            for half in range(BF16_ROWS // SUBLANES):
                r0 = g * BF16_ROWS + half * SUBLANES
                gate = None
                for h in range(PEER_HEADS):
                    ssum = rep_s[0, h, r] + s2_s[h, r0:r0 + SUBLANES, :]
                    prod = rep_s[1, h, r] * e2_s[h, r0:r0 + SUBLANES, :]
                    term = jnp.where(ssum >= tau_s[h], prod, 0.0)
                    gate = term if gate is None else gate + term
                a = at_ref[r * N_KEYS + r0:r * N_KEYS + r0 + SUBLANES, :]
                halves.append(gate * ((0.5 * a) * (1.0 + lax.erf(a * SQRT_HALF))))
            row0 = r * N_KEYS + g * BF16_ROWS
            coef_ref[row0:row0 + BF16_ROWS, :] = jnp.concatenate(halves, axis=0).astype(jnp.bfloat16)


def _peer_kernel(x_ref, g_ref, wqt_ref, keys_ref, u0_ref, u_ref, vt_ref, vtl_ref, o_ref,
                 ht_s, s1_s, s2_s, e1_s, e2_s, tau_s, top_s, rep_s,
                 at_a, at_b, coef_a, coef_b, acc_s):
    j = pl.program_id(1)
    n_chunks = pl.num_programs(1)

    @pl.when(j == 0)
    def _first():
        _peer_prologue(x_ref, g_ref, wqt_ref, keys_ref, ht_s, s1_s, s2_s, e1_s, e2_s, tau_s, top_s)
        at_a[...] = jnp.dot(u0_ref[...], ht_s[...], preferred_element_type=jnp.float32)
        coef_b[...] = jnp.zeros_like(coef_b)
        acc_s[...] = jnp.zeros_like(acc_s)

    def step(at_cur, at_nxt, coef_cur, coef_prv):
        chunk, tb = at_cur.shape
        d = acc_s.shape[0]

        def at_piece(mi, ni):
            m = slice(mi * chunk // 2, (mi + 1) * chunk // 2)
            n = slice(ni * tb // 2, (ni + 1) * tb // 2)
            at_nxt[m, n] = jnp.dot(u_ref[m, :], ht_s[:, n], preferred_element_type=jnp.float32)

        def acc_piece(mi, ni):
            m = slice(mi * d // 2, (mi + 1) * d // 2)
            n = slice(ni * tb // 2, (ni + 1) * tb // 2)
            acc_s[m, n] += jnp.dot(vt_ref[m, :], coef_prv[:, n], preferred_element_type=jnp.float32)

        pieces = [functools.partial(f, mi, ni) for ni in range(2) for mi in range(2) for f in (at_piece, acc_piece)]
        _peer_gate_chunk(j, s1_s, s2_s, e1_s, e2_s, tau_s, rep_s, at_cur, coef_cur,
                         between=lambda r: pieces[r]())

    @pl.when(j % 2 == 0)
    def _even():
        step(at_a, at_b, coef_a, coef_b)

    @pl.when(j % 2 == 1)
    def _odd():
        step(at_b, at_a, coef_b, coef_a)

    @pl.when(j == n_chunks - 1)
    def _last():
        acc = acc_s[...] + jnp.dot(vtl_ref[...], coef_b[...], preferred_element_type=jnp.float32)
        o_ref[...] = x_ref[...] + acc.T


def _peer_residual(x, g, w_q, sub_keys, u_tab, v_tab, *, tokens=PEER_TOKENS, chunk=PEER_EXPERT_CHUNK):
    n, d = x.shape
    n_chunks = N_EXPERTS // chunk
    assert n % tokens == 0 and N_EXPERTS % chunk == 0 and chunk == SUBLANES * N_KEYS and n_chunks % 2 == 0
    wqt = w_q.T.astype(jnp.bfloat16)
    keys = sub_keys.astype(jnp.bfloat16)
    u_b = u_tab.astype(jnp.bfloat16)
    vt_b = v_tab.T.astype(jnp.bfloat16)
    f32 = jnp.float32
    once = pl.Buffered(1)
    per_head = pltpu.VMEM((PEER_HEADS, N_KEYS, tokens), f32)
    at_buf = pltpu.VMEM((chunk, tokens), f32)
    coef_buf = pltpu.VMEM((chunk, tokens), jnp.bfloat16)
    return pl.pallas_call(
        _peer_kernel,
        grid=(n // tokens, n_chunks),
        in_specs=[
            pl.BlockSpec((tokens, d), lambda i, j: (i, 0)),
            pl.BlockSpec((1, d), lambda i, j: (0, 0), pipeline_mode=once),
            pl.BlockSpec((PEER_HEADS * PEER_DKEY, d), lambda i, j: (0, 0), pipeline_mode=once),
            pl.BlockSpec((PEER_HEADS, 2, N_KEYS, PEER_HALF), lambda i, j: (0, 0, 0, 0), pipeline_mode=once),
            pl.BlockSpec((chunk, d), lambda i, j: (0, 0), pipeline_mode=once),
            pl.BlockSpec((chunk, d), lambda i, j: (jnp.minimum(j + 1, n_chunks - 1), 0)),
            pl.BlockSpec((d, chunk), lambda i, j: (0, jnp.maximum(j - 1, 0))),
            pl.BlockSpec((d, chunk), lambda i, j: (0, n_chunks - 1), pipeline_mode=once),
        ],
        out_specs=pl.BlockSpec((tokens, d), lambda i, j: (i, 0)),
        out_shape=jax.ShapeDtypeStruct((n, d), f32),
        scratch_shapes=[
            pltpu.VMEM((d, tokens), jnp.bfloat16),
            per_head, per_head, per_head, per_head,
            pltpu.VMEM((PEER_HEADS, SUBLANES, tokens), f32),
            pltpu.VMEM((2, PEER_TOPK, tokens), f32),
            pltpu.VMEM((2, PEER_HEADS, SUBLANES, SUBLANES, tokens), f32),
            at_buf, at_buf, coef_buf, coef_buf,
            pltpu.VMEM((d, tokens), f32),
        ],
        compiler_params=pltpu.CompilerParams(
            dimension_semantics=("arbitrary", "arbitrary"),
            vmem_limit_bytes=PEER_VMEM_LIMIT),
        name="peer",
    )(x, g.reshape(1, d), wqt, keys, u_b, u_b, vt_b, vt_b)


def _flash_kernel(q_ref, k_ref, v_ref, o_ref, *, scale, block):
    qi = pl.program_id(2)
    q = q_ref[0, 0]

    def kv_block(kb, carry, diagonal):
        m, l, acc = carry
        k0 = pl.multiple_of(kb * block, block)
        k = k_ref[0, 0, pl.ds(k0, block), :]
        v = v_ref[0, 0, pl.ds(k0, block), :]
        s = lax.dot_general(q, k, (((1,), (1,)), ((), ())), preferred_element_type=jnp.float32) * scale
        if diagonal:
            row = lax.broadcasted_iota(jnp.int32, s.shape, 0)
            col = lax.broadcasted_iota(jnp.int32, s.shape, 1)
            s = jnp.where(row >= col, s, NEG_INF)
        m_new = jnp.maximum(m, jnp.max(s, axis=1, keepdims=True))
        alpha = jnp.exp(m - m_new)
        p = jnp.exp(s - m_new)
        l_new = alpha * l + jnp.sum(p, axis=1, keepdims=True)
        acc_new = alpha * acc + jnp.dot(p.astype(v.dtype), v, preferred_element_type=jnp.float32)
        return m_new, l_new, acc_new

    init = (jnp.full((block, 1), NEG_INF, jnp.float32), jnp.zeros((block, 1), jnp.float32),
            jnp.zeros((block, v_ref.shape[3]), jnp.float32))
    carry = lax.fori_loop(0, qi, lambda kb, c: kv_block(kb, c, False), init)
    _, l, acc = kv_block(qi, carry, True)
    o_ref[0, 0] = acc / l


def _flash_attention(q, k, v, scale, block=FLASH_BLOCK):
    b, h, s, d_qk = q.shape
    d_v = v.shape[3]
    assert s % block == 0
    return pl.pallas_call(
        functools.partial(_flash_kernel, scale=scale, block=block),
        grid=(b, h, s // block),
        in_specs=[
            pl.BlockSpec((1, 1, block, d_qk), lambda bi, hi, qi: (bi, hi, qi, 0)),
            pl.BlockSpec((1, 1, s, d_qk), lambda bi, hi, qi: (bi, hi, 0, 0)),
            pl.BlockSpec((1, 1, s, d_v), lambda bi, hi, qi: (bi, hi, 0, 0)),
        ],
        out_specs=pl.BlockSpec((1, 1, block, d_v), lambda bi, hi, qi: (bi, hi, qi, 0)),
        out_shape=jax.ShapeDtypeStruct((b, h, s, d_v), jnp.float32),
        compiler_params=pltpu.CompilerParams(
            dimension_semantics=("arbitrary", "arbitrary", "arbitrary"),
            vmem_limit_bytes=FLASH_VMEM_LIMIT),
        name="mla_prompt_attention",
    )(q, k, v)


def _mlstm_kernel(q_ref, k_ref, v_ref, col_ref, row_ref, h_ref, c_ref, n_ref, m_ref):
    bsz, chunk, _ = q_ref.shape
    bf16 = jnp.bfloat16
    nt = (((1,), (1,)), ((), ()))

    @pl.when(pl.program_id(0) == 0)
    def _init():
        c_ref[...] = jnp.zeros_like(c_ref)
        n_ref[...] = jnp.zeros_like(n_ref)
        m_ref[...] = jnp.full_like(m_ref, M_INIT)

    t_idx = lax.broadcasted_iota(jnp.int32, (chunk, chunk), 0)
    s_idx = lax.broadcasted_iota(jnp.int32, (chunk, chunk), 1)
    causal = s_idx <= t_idx
    for b in range(bsz):
        for h in range(ML_HEADS):
            cols = slice(h * ML_DQK, (h + 1) * ML_DQK)
            qh, kh, vh = q_ref[b, :, cols], k_ref[b, :, cols], v_ref[b, :, cols]
            a_col = col_ref[b, :, h:h + 1]
            b_col = col_ref[b, :, ML_HEADS + h:ML_HEADS + h + 1]
            g_col = col_ref[b, :, 2 * ML_HEADS + h:2 * ML_HEADS + h + 1]
            a_row = row_ref[b, h:h + 1, :]
            m_prev = m_ref[b, h][:, 0:1]
            c_prev = c_ref[b, h]
            n_prev = n_ref[b, h]
            mg = jnp.maximum(m_prev, g_col)
            w = jnp.exp(jnp.where(causal, a_row - mg, NEG_INF))
            inter = jnp.exp(m_prev - mg)
            qb = qh.astype(bf16)
            vb = vh.astype(bf16)
            wqk = w * lax.dot_general(qb, kh.astype(bf16), nt, preferred_element_type=jnp.float32)
            num = (jnp.dot(wqk.astype(bf16), vb, preferred_element_type=jnp.float32)
                   + inter * jnp.dot(qb, c_prev.astype(bf16), preferred_element_type=jnp.float32))
            den = (jnp.sum(wqk, axis=1, keepdims=True)
                   + inter * jnp.sum(qh * n_prev, axis=1, keepdims=True))
            h_ref[b, :, cols] = num / jnp.maximum(jnp.abs(den), jnp.exp(-(b_col + mg)))
            b_last = b_col[chunk - 1:chunk, :]
            m_last = b_last + mg[chunk - 1:chunk, :]
            f_last = jnp.exp(m_prev + b_last - m_last)
            kw = jnp.exp(a_col + b_last - m_last) * kh
            c_ref[b, h] = f_last * c_prev + jnp.dot(kw.T.astype(bf16), vb, preferred_element_type=jnp.float32)
            n_ref[b, h] = f_last * n_prev + jnp.sum(kw, axis=0, keepdims=True)
            m_ref[b, h] = jnp.broadcast_to(m_last, m_ref.shape[2:])


def _mlstm_prompt(q, k, v, i_pre, f_pre):
    bsz, s, hd = q.shape
    chunk = ML_CHUNK
    nc = s // chunk
    assert s % chunk == 0
    f32 = jnp.float32
    lf = jax.nn.log_sigmoid(f_pre).reshape(bsz, nc, chunk, ML_HEADS)
    b_cum = jnp.cumsum(lf, axis=2)
    a = i_pre.reshape(bsz, nc, chunk, ML_HEADS) - b_cum
    g = lax.cummax(a, axis=2)
    col = jnp.concatenate([a, b_cum, g], axis=-1).reshape(bsz, s, 3 * ML_HEADS)
    row = jnp.transpose(a.reshape(bsz, s, ML_HEADS), (0, 2, 1))
    blk = pl.BlockSpec((bsz, chunk, hd), lambda c: (0, c, 0))
    state = lambda shape: pl.BlockSpec(shape, lambda c: (0,) * len(shape))
    c_shape = (bsz, ML_HEADS, ML_DQK, ML_DV)
    n_shape = (bsz, ML_HEADS, 1, ML_DQK)
    h, c_f, n_f, m_f = pl.pallas_call(
        _mlstm_kernel,
        grid=(nc,),
        in_specs=[blk, blk, blk,
                  pl.BlockSpec((bsz, chunk, 3 * ML_HEADS), lambda c: (0, c, 0)),
                  pl.BlockSpec((bsz, ML_HEADS, chunk), lambda c: (0, 0, c))],
        out_specs=[blk, state(c_shape), state(n_shape), state(n_shape)],
        out_shape=[jax.ShapeDtypeStruct((bsz, s, hd), f32), jax.ShapeDtypeStruct(c_shape, f32),
                   jax.ShapeDtypeStruct(n_shape, f32), jax.ShapeDtypeStruct(n_shape, f32)],
        compiler_params=pltpu.CompilerParams(dimension_semantics=("arbitrary",)),
        name="mlstm_prompt",
    )(q, k, v, col, row)
    return h.reshape(bsz, s, ML_HEADS, ML_DV), c_f, n_f[:, :, 0, :], m_f[:, :, 0, 0]


def _decode_kernel(pt_ref, ql_ref, qr_ref, cn_ref, kn_ref, *refs, scale, pages):
    c_refs = refs[:pages]
    k_refs = refs[pages:2 * pages]
    o_ref = refs[2 * pages]
    m_s, l_s, acc_s = refs[2 * pages + 1:]
    g = pl.program_id(1)
    bf16 = jnp.bfloat16
    nt = (((1,), (1,)), ((), ()))
    ql = ql_ref[0]
    qr = qr_ref[0]

    @pl.when(g == 0)
    def _init():
        m_s[...] = jnp.full_like(m_s, NEG_INF)
        l_s[...] = jnp.zeros_like(l_s)
        acc_s[...] = jnp.zeros_like(acc_s)

    def update(s, c):
        m_new = jnp.maximum(m_s[...], jnp.max(s, axis=1, keepdims=True))
        alpha = jnp.exp(m_s[...] - m_new)
        p = jnp.exp(s - m_new)
        l_s[...] = alpha * l_s[...] + jnp.sum(p, axis=1, keepdims=True)
        acc_s[...] = alpha * acc_s[...] + jnp.dot(p.astype(bf16), c, preferred_element_type=jnp.float32)
        m_s[...] = m_new

    c = jnp.concatenate([r[0] for r in c_refs], axis=0).astype(bf16)
    kr = jnp.concatenate([r[0] for r in k_refs], axis=0).astype(bf16)
    s = (lax.dot_general(ql, c, nt, preferred_element_type=jnp.float32)
         + lax.dot_general(qr, kr, nt, preferred_element_type=jnp.float32)) * scale
    update(s, c)

    @pl.when(g == pl.num_programs(1) - 1)
    def _finish():
        cn = cn_ref[0].astype(bf16)
        kn = kn_ref[0].astype(bf16)
        sn = (lax.dot_general(ql, cn, nt, preferred_element_type=jnp.float32)
              + lax.dot_general(qr, kn, nt, preferred_element_type=jnp.float32)) * scale
        t_of_row = lax.broadcasted_iota(jnp.int32, sn.shape, 0) // MLA_HEADS
        key = lax.broadcasted_iota(jnp.int32, sn.shape, 1)
        update(jnp.where(key <= t_of_row, sn, NEG_INF), cn)
        o_ref[0] = acc_s[...] / l_s[...]


def _mla_decode(q_lat, q_rope, c_new, kr_new, cache_latent, cache_rope, page_table, scale,
                pages=DECODE_PAGES):
    b, rows, _ = q_lat.shape
    t = c_new.shape[1]
    n_pages = page_table.shape[1]
    assert n_pages % pages == 0 and t <= DEC_PAD
    bf16 = jnp.bfloat16
    pad = ((0, 0), (0, DEC_PAD - t), (0, 0))

    def page_spec(width, i):
        return pl.BlockSpec((1, PAGE_SIZE, width), lambda bi, g, pt: (pt[bi, g * pages + i], 0, 0))

    grid_spec = pltpu.PrefetchScalarGridSpec(
        num_scalar_prefetch=1,
        grid=(b, n_pages // pages),
        in_specs=[
            pl.BlockSpec((1, rows, KV_LORA), lambda bi, g, pt: (bi, 0, 0)),
            pl.BlockSpec((1, rows, QK_ROPE), lambda bi, g, pt: (bi, 0, 0)),
            pl.BlockSpec((1, DEC_PAD, KV_LORA), lambda bi, g, pt: (bi, 0, 0)),
            pl.BlockSpec((1, DEC_PAD, QK_ROPE), lambda bi, g, pt: (bi, 0, 0)),
        ] + [page_spec(KV_LORA, i) for i in range(pages)] + [page_spec(QK_ROPE, i) for i in range(pages)],
        out_specs=pl.BlockSpec((1, rows, KV_LORA), lambda bi, g, pt: (bi, 0, 0)),
        scratch_shapes=[pltpu.VMEM((rows, 1), jnp.float32), pltpu.VMEM((rows, 1), jnp.float32),
                        pltpu.VMEM((rows, KV_LORA), jnp.float32)],
    )
    return pl.pallas_call(
        functools.partial(_decode_kernel, scale=scale, pages=pages),
        grid_spec=grid_spec,
        out_shape=jax.ShapeDtypeStruct((b, rows, KV_LORA), jnp.float32),
        compiler_params=pltpu.CompilerParams(
            dimension_semantics=("arbitrary", "arbitrary"),
            vmem_limit_bytes=FLASH_VMEM_LIMIT),
        name="mla_decode_attention",
    )(page_table, q_lat.astype(bf16), q_rope.astype(bf16), jnp.pad(c_new, pad), jnp.pad(kr_new, pad),
      *([cache_latent] * pages), *([cache_rope] * pages))


def _rmsnorm(x, g):
    xf = x.astype(jnp.float32)
    y = xf * lax.rsqrt(jnp.mean(xf * xf, axis=-1, keepdims=True) + NORM_EPS)
    return (y * g.astype(jnp.float32)).astype(x.dtype)


def _layernorm(x, g, b):
    xf = x.astype(jnp.float32)
    mu = jnp.mean(xf, axis=-1, keepdims=True)
    var = jnp.mean(jnp.square(xf - mu), axis=-1, keepdims=True)
    y = (xf - mu) * lax.rsqrt(var + NORM_EPS)
    return (y * g.astype(jnp.float32) + b.astype(jnp.float32)).astype(x.dtype)


def _rope(x, pos):
    half = QK_ROPE // 2
    inv = ROPE_THETA ** (-jnp.arange(half, dtype=jnp.float32) / half)
    ang = pos.astype(jnp.float32)[:, None] * inv[None, :]
    ang = ang.reshape((ang.shape[0],) + (1,) * (x.ndim - 3) + (half,))
    cos, sin = jnp.cos(ang), jnp.sin(ang)
    xf = x.astype(jnp.float32)
    x1, x2 = xf[..., :half], xf[..., half:]
    return jnp.concatenate([x1 * cos - x2 * sin, x1 * sin + x2 * cos], axis=-1).astype(x.dtype)


def _even_project(h, pos, w_in, q_norm, w_uq, kv_norm, b_i, b_f):
    B, S, _ = h.shape
    f32 = jnp.float32
    z = h @ w_in
    cq, ckv, mq, mk, mv, mi, mf, mo = jnp.split(z, IN_SPLITS, axis=-1)
    q = (_rmsnorm(cq, q_norm) @ w_uq).reshape(B, S, MLA_HEADS, QK_NOPE + QK_ROPE)
    q_nope = q[..., :QK_NOPE]
    q_rope = _rope(q[..., QK_NOPE:], pos)
    c_kv = _rmsnorm(ckv[..., :KV_LORA], kv_norm)
    k_rope = _rope(ckv[..., KV_LORA:], pos)
    mq = mq.reshape(B, S, ML_HEADS, ML_DQK).astype(f32)
    mk = mk.reshape(B, S, ML_HEADS, ML_DQK).astype(f32) * (ML_DQK ** -0.5)
    mv = mv.reshape(B, S, ML_HEADS, ML_DV).astype(f32)
    i_pre = mi.astype(f32) + b_i.astype(f32)
    f_pre = mf.astype(f32) + b_f.astype(f32)
    return q_nope, q_rope, c_kv, k_rope, mq, mk, mv, i_pre, f_pre, mo


def _mla_prompt(q_nope, q_rope, c_kv, k_rope, w_uk, w_uv):
    B, S = q_nope.shape[0], q_nope.shape[1]
    bf16 = jnp.bfloat16
    scale = (QK_NOPE + QK_ROPE) ** -0.5
    k_nope = jnp.einsum('bsc,chd->bshd', c_kv, w_uk)
    v = jnp.einsum('bsc,chd->bshd', c_kv, w_uv)
    q = jnp.concatenate([q_nope, q_rope], axis=-1)
    k = jnp.concatenate([k_nope, jnp.broadcast_to(k_rope[:, :, None, :], (B, S, MLA_HEADS, QK_ROPE))], axis=-1)
    heads_major = lambda a: jnp.transpose(a, (0, 2, 1, 3)).astype(bf16)
    o = _flash_attention(heads_major(q), heads_major(k), heads_major(v), scale)
    return jnp.transpose(o, (0, 2, 1, 3))


def _mla_sample(q_nope, q_rope, c_new, kr_new, cache_latent, cache_rope, page_table, w_uk, w_uv):
    B, T = q_nope.shape[0], q_nope.shape[1]
    scale = (QK_NOPE + QK_ROPE) ** -0.5
    q_lat = jnp.einsum('bthd,chd->bthc', q_nope, w_uk)
    o_lat = _mla_decode(q_lat.reshape(B, T * MLA_HEADS, KV_LORA), q_rope.reshape(B, T * MLA_HEADS, QK_ROPE),
                        c_new, kr_new, cache_latent, cache_rope, page_table, scale)
    return jnp.einsum('bthc,chd->bthd', o_lat.reshape(B, T, MLA_HEADS, KV_LORA), w_uv)


def _mlstm_chunked(q, k, v, i_pre, f_pre, C0, n0, m0):
    B, S, H, _ = q.shape
    L = ML_CHUNK if S % ML_CHUNK == 0 else S
    nc = S // L
    logf = jax.nn.log_sigmoid(f_pre)

    def chunks(a):
        return jnp.moveaxis(a.reshape((B, nc, L) + a.shape[2:]), 1, 0)

    causal = jnp.tril(jnp.ones((L, L), dtype=bool))[None, :, :, None]

    def step(carry, inp):
        C, n, m = carry
        qc, kc, vc, ic, lfc = inp
        b = jnp.cumsum(lfc, axis=1)
        a = ic - b
        m_t = b + jnp.maximum(m[:, None, :], lax.cummax(a, axis=1))
        d_log = (b - m_t)[:, :, None, :] + a[:, None, :, :]
        w = jnp.exp(jnp.where(causal, d_log, -jnp.inf))
        inter = jnp.exp(m[:, None, :] + b - m_t)
        wqk = w * jnp.einsum('bthd,bshd->btsh', qc, kc)
        num = (jnp.einsum('btsh,bshv->bthv', wqk, vc)
               + inter[..., None] * jnp.einsum('bthd,bhdv->bthv', qc, C))
        den = wqk.sum(axis=2) + inter * jnp.einsum('bthd,bhd->bth', qc, n)
        h = num / jnp.maximum(jnp.abs(den), jnp.exp(-m_t))[..., None]
        m_last = m_t[:, -1]
        w_s = jnp.exp(a + b[:, -1:, :] - m_last[:, None, :])
        f_last = jnp.exp(m + b[:, -1] - m_last)
        C_new = f_last[..., None, None] * C + jnp.einsum('bsh,bshd,bshv->bhdv', w_s, kc, vc)
        n_new = f_last[..., None] * n + jnp.einsum('bsh,bshd->bhd', w_s, kc)
        return (C_new, n_new, m_last), h

    (C_f, n_f, m_f), hs = lax.scan(step, (C0, n0, m0),
                                   (chunks(q), chunks(k), chunks(v), chunks(i_pre), chunks(logf)))
    h = jnp.moveaxis(hs, 0, 1).reshape(B, S, H, v.shape[-1])
    return h, C_f, n_f, m_f


def _even_merge(att, h_ml, o_pre, ml_norm, w_out):
    B, S = att.shape[0], att.shape[1]
    hn = h_ml * lax.rsqrt(jnp.mean(h_ml * h_ml, axis=-1, keepdims=True) + NORM_EPS)
    y_ml = (hn.reshape(B, S, ML_HEADS * ML_DV) * ml_norm.astype(jnp.float32)
            * jax.nn.sigmoid(o_pre.astype(jnp.float32))).astype(att.dtype)
    mix = jnp.concatenate([att.reshape(B, S, MLA_HEADS * V_HEAD), y_ml], axis=-1)
    return mix @ w_out


def _conformer_conv(h, buf, w_pw1, b_pw1, w_dw, b_dw, ln_g, ln_b, w_pw2, b_pw2):
    a = h @ w_pw1 + b_pw1
    g = a[..., :D_MODEL] * jax.nn.sigmoid(a[..., D_MODEL:])
    full = jnp.concatenate([buf.astype(g.dtype), g], axis=1)
    y = lax.conv_general_dilated(full, w_dw[:, None, :].astype(full.dtype), window_strides=(1,),
                                 padding='VALID', dimension_numbers=('NWC', 'WIO', 'NWC'),
                                 feature_group_count=D_MODEL) + b_dw
    y = jax.nn.silu(_layernorm(y, ln_g, ln_b))
    return y @ w_pw2 + b_pw2, full[:, -(CONV_W - 1):]


def kernel(x_prompt, x_sample, cache_mla_latent, cache_mla_rope, state_mlstm_C, state_mlstm_n, state_mlstm_m, state_conv, page_table, attn_norm, w_in, q_norm, w_uq, kv_norm, w_uk, w_uv, b_igate, b_fgate, ml_norm, w_out, conv_norm, w_pw1, b_pw1, w_dw, b_dw, conv_ln_g, conv_ln_b, w_pw2, b_pw2, ffn_norm, peer_wq, peer_keys, peer_u, peer_v, final_norm):
    f32 = jnp.float32
    bp, sp = x_prompt.shape[0], x_prompt.shape[1]
    bs, ss = x_sample.shape[0], x_sample.shape[1]
    n_p, n_s = bp * sp, bs * ss
    pos_p = jnp.arange(sp, dtype=jnp.int32)
    pos_s = PAST_LEN + jnp.arange(ss, dtype=jnp.int32)
    past = page_table.shape[1] * PAGE_SIZE
    xp, xs = x_prompt, x_sample
    lat_p, rope_p, lat_s, rope_s = [], [], [], []
    cp_l, np_l, mp_l, cs_l, ns_l, ms_l = [], [], [], [], [], []
    convp_l, convs_l = [], []
    for layer in range(DEPTH):
        j = layer // 2
        if layer % 2 == 0:
            (qn_p, qr_p, c_p, kr_p, mq_p, mk_p, mv_p, ig_p, fg_p, og_p) = _even_project(
                _rmsnorm(xp, attn_norm[j]), pos_p, w_in[j], q_norm[j], w_uq[j], kv_norm[j], b_igate[j], b_fgate[j])
            (qn_s, qr_s, c_s, kr_s, mq_s, mk_s, mv_s, ig_s, fg_s, og_s) = _even_project(
                _rmsnorm(xs, attn_norm[j]), pos_s, w_in[j], q_norm[j], w_uq[j], kv_norm[j], b_igate[j], b_fgate[j])
            att_p = _mla_prompt(qn_p, qr_p, c_p, kr_p, w_uk[j], w_uv[j])
            att_s = _mla_sample(qn_s, qr_s, c_s, kr_s, cache_mla_latent[j], cache_mla_rope[j], page_table,
                                w_uk[j], w_uv[j])
            h_p, C_p, n_p_, m_p = _mlstm_prompt(mq_p.reshape(bp, sp, -1), mk_p.reshape(bp, sp, -1),
                                                mv_p.reshape(bp, sp, -1), ig_p, fg_p)
            h_s, C_s, n_s_, m_s = _mlstm_chunked(mq_s, mk_s, mv_s, ig_s, fg_s,
                                                 state_mlstm_C[j].astype(f32), state_mlstm_n[j].astype(f32),
                                                 state_mlstm_m[j].astype(f32))
            xp = xp + _even_merge(att_p, h_p, og_p, ml_norm[j], w_out[j])
            xs = xs + _even_merge(att_s, h_s, og_s, ml_norm[j], w_out[j])
            lat_p.append(c_p)
            rope_p.append(kr_p)
            lat_s.append(c_s)
            rope_s.append(kr_s)
            cp_l.append(C_p)
            np_l.append(n_p_)
            mp_l.append(m_p)
            cs_l.append(C_s)
            ns_l.append(n_s_)
            ms_l.append(m_s)
        else:
            buf0 = jnp.zeros((bp, CONV_W - 1, D_MODEL), xp.dtype)
            out_p, buf_p = _conformer_conv(_rmsnorm(xp, conv_norm[j]), buf0, w_pw1[j], b_pw1[j], w_dw[j], b_dw[j],
                                           conv_ln_g[j], conv_ln_b[j], w_pw2[j], b_pw2[j])
            out_s, buf_s = _conformer_conv(_rmsnorm(xs, conv_norm[j]), state_conv[j], w_pw1[j], b_pw1[j], w_dw[j],
                                           b_dw[j], conv_ln_g[j], conv_ln_b[j], w_pw2[j], b_pw2[j])
            xp = xp + out_p
            xs = xs + out_s
            convp_l.append(buf_p)
            convs_l.append(buf_s)
        rows = jnp.concatenate([xp.reshape(n_p, D_MODEL), xs.reshape(n_s, D_MODEL)], axis=0)
        rows = _peer_residual(rows, ffn_norm[layer], peer_wq[layer], peer_keys[layer],
                              peer_u[layer], peer_v[layer])
        xp = rows[:n_p].reshape(bp, sp, D_MODEL)
        xs = rows[n_p:].reshape(bs, ss, D_MODEL)
    y_prompt = _rmsnorm(xp, final_norm)
    y_sample = _rmsnorm(xs, final_norm)
    return (y_prompt, y_sample, jnp.stack(lat_p), jnp.stack(rope_p), jnp.stack(lat_s), jnp.stack(rope_s),
            jnp.stack(cp_l), jnp.stack(np_l), jnp.stack(mp_l), jnp.stack(cs_l), jnp.stack(ns_l), jnp.stack(ms_l),
            jnp.stack(convp_l), jnp.stack(convs_l))
```

```python
import functools
import math

import jax
import jax.numpy as jnp
import numpy as np
from jax import lax
from jax.experimental import pallas as pl
from jax.experimental.pallas import tpu as pltpu

D_MODEL = 1024
DEPTH = 2
PAST_LEN = 8192
PAGE_SIZE = 128
NORM_EPS = 1e-6

MLA_HEADS = 8
Q_LORA = 384
KV_LORA = 256
QK_NOPE = 64
QK_ROPE = 32
V_HEAD = 64
ROPE_THETA = 10000.0
Q_BLOCK = 128

ML_HEADS = 4
ML_DQK = 128
ML_DV = 128
ML_CHUNK = 128
M_INIT = -1e30

IN_WIDTHS = (Q_LORA, KV_LORA + QK_ROPE, ML_HEADS * ML_DQK, ML_HEADS * ML_DQK,
             ML_HEADS * ML_DV, ML_HEADS, ML_HEADS, ML_HEADS * ML_DV)
IN_SPLITS = tuple(int(c) for c in np.cumsum(IN_WIDTHS)[:-1])

CONV_W = 31

PEER_HEADS = 8
N_KEYS = 128
N_EXPERTS = N_KEYS * N_KEYS
PEER_DKEY = 256
PEER_HALF = PEER_DKEY // 2
PEER_TOPK = 16

SUBLANES = 8
LANES = 128
BF16_ROWS = 16
MATMUL_BLOCK_M = 512
MATMUL_BLOCK_N = 1408
PEER_TOKENS = 512
PEER_EXPERT_CHUNK = 1024
PEER_SORT_LANES = 256
PEER_VMEM_LIMIT = 56 * 1024 * 1024
FLASH_BLOCK = 512
DECODE_PAGES = 32
DEC_PAD = 16
FLASH_VMEM_LIMIT = 32 * 1024 * 1024

NEG_INF = float("-inf")
SQRT_HALF = math.sqrt(0.5)


def _peer_candidates(v1, v2):
    t = v1.shape[1]
    row = lax.broadcasted_iota(jnp.int32, (8, t), 0)
    slabs = [v1[0:8] + v2[0:1], v1[8:16] + v2[0:1], v1[0:8] + v2[1:2]]
    for b in range(2, 8):
        n_a = PEER_TOPK // (b + 1)
        slabs.append(jnp.where(row < n_a, v1[0:8] + v2[b:b + 1], NEG_INF))
    slabs.append(v1[0:1] + v2[8:16])
    return jnp.concatenate(slabs, axis=0)


def _extract_top(cur, n):
    vals = []
    for i in range(n):
        m = jnp.max(cur, axis=0, keepdims=True)
        vals.append(m)
        if i + 1 < n:
            cur = jnp.where(cur == m, NEG_INF, cur)
    return vals


def _merge_exchange_pairs(n):
    pairs = []
    p = 1
    while p < n:
        k = p
        while k >= 1:
            for j in range(k % p, n - k, 2 * k):
                for i in range(min(k, n - j - k)):
                    if (i + j) // (2 * p) == (i + j + k) // (2 * p):
                        pairs.append((i + j, i + j + k))
            k //= 2
        p *= 2
    return pairs


def _top_of_rows(s_ref, head, n, out_ref, lane_block):
    tiles = N_KEYS // SUBLANES
    for l0 in range(0, s_ref.shape[2], lane_block):
        lanes = slice(l0, l0 + lane_block)
        v = [s_ref[head, i * SUBLANES:(i + 1) * SUBLANES, lanes] for i in range(tiles)]
        for i, k in _merge_exchange_pairs(tiles):
            v[i], v[k] = jnp.maximum(v[i], v[k]), jnp.minimum(v[i], v[k])
        for r in range(n):
            m = jnp.max(v[0], axis=0, keepdims=True)
            out_ref[r:r + 1, lanes] = m
            if r + 1 < n:
                hit = v[0] == m
                for k in range(min(tiles, n) - 1 - r):
                    v[k] = jnp.where(hit, v[k + 1], v[k])


def _peer_prologue(x_ref, g_ref, wqt_ref, keys_ref, ht_s, s1_s, s2_s, e1_s, e2_s, tau_s, top_s):
    tb = x_ref.shape[0]
    x = x_ref[...]
    ms = jnp.mean(x * x, axis=-1, keepdims=True)
    hn = (x * lax.rsqrt(ms + NORM_EPS)) * g_ref[...]
    ht = hn.T.astype(jnp.bfloat16)
    ht_s[...] = ht
    qtb = jnp.dot(wqt_ref[...], ht, preferred_element_type=jnp.float32).astype(jnp.bfloat16)
    for h in range(PEER_HEADS):
        tops = []
        for p, s_ref in enumerate((s1_s, s2_s)):
            r0 = (h * 2 + p) * PEER_HALF
            s = jnp.dot(keys_ref[h, p], qtb[r0:r0 + PEER_HALF, :],
                        preferred_element_type=jnp.float32)
            s_ref[h] = s
            _top_of_rows(s_ref, h, PEER_TOPK, top_s.at[p], PEER_SORT_LANES)
            tops.append(top_s[p, 0:1, :])
        sc = _extract_top(_peer_candidates(top_s[0], top_s[1]), PEER_TOPK)
        z = jnp.zeros_like(sc[0])
        for k in range(PEER_TOPK):
            z = z + jnp.exp(sc[k] - sc[0])
        tau_s[h] = jnp.broadcast_to(sc[PEER_TOPK - 1], (SUBLANES, tb))
        e1_s[h] = jnp.exp(s1_s[h] - tops[0]) / z
        e2_s[h] = jnp.exp(s2_s[h] - tops[1])


def _peer_gate_chunk(j, s1_s, s2_s, e1_s, e2_s, tau_s, rep_s, at_ref, coef_ref, between=None):
    rows = at_ref.shape[0] // N_KEYS
    tb = at_ref.shape[1]
    e1_0 = pl.multiple_of(j * rows, SUBLANES)
    for h in range(PEER_HEADS):
        s1_blk = s1_s[h, pl.ds(e1_0, rows), :]
        e1_blk = e1_s[h, pl.ds(e1_0, rows), :]
        for r in range(rows):
            rep_s[0, h, r] = jnp.broadcast_to(s1_blk[r:r + 1, :], (SUBLANES, tb))
            rep_s[1, h, r] = jnp.broadcast_to(e1_blk[r:r + 1, :], (SUBLANES, tb))
    for r in range(rows):
        if between is not None:
            between(r)
        for g in range(N_KEYS // BF16_ROWS):
            halves = []
            for half in range(BF16_ROWS // SUBLANES):
                r0 = g * BF16_ROWS + half * SUBLANES
                gate = None
                for h in range(PEER_HEADS):
                    ssum = rep_s[0, h, r] + s2_s[h, r0:r0 + SUBLANES, :]
                    prod = rep_s[1, h, r] * e2_s[h, r0:r0 + SUBLANES, :]
                    term = jnp.where(ssum >= tau_s[h], prod, 0.0)
                    gate = term if gate is None else gate + term
                a = at_ref[r * N_KEYS + r0:r * N_KEYS + r0 + SUBLANES, :]
                halves.append(gate * ((0.5 * a) * (1.0 + lax.erf(a * SQRT_HALF))))
            row0 = r * N_KEYS + g * BF16_ROWS
            coef_ref[row0:row0 + BF16_ROWS, :] = jnp.concatenate(halves, axis=0).astype(jnp.bfloat16)


def _peer_kernel(x_ref, g_ref, wqt_ref, keys_ref, u0_ref, u_ref, vt_ref, vtl_ref, o_ref,
                 ht_s, s1_s, s2_s, e1_s, e2_s, tau_s, top_s, rep_s,
                 at_a, at_b, coef_a, coef_b, acc_s):
    j = pl.program_id(1)
    n_chunks = pl.num_programs(1)

    @pl.when(j == 0)
    def _first():
        _peer_prologue(x_ref, g_ref, wqt_ref, keys_ref, ht_s, s1_s, s2_s, e1_s, e2_s, tau_s, top_s)
        at_a[...] = jnp.dot(u0_ref[...], ht_s[...], preferred_element_type=jnp.float32)
        coef_b[...] = jnp.zeros_like(coef_b)
        acc_s[...] = jnp.zeros_like(acc_s)

    def step(at_cur, at_nxt, coef_cur, coef_prv):
        chunk, tb = at_cur.shape
        d = acc_s.shape[0]

        def at_piece(mi, ni):
            m = slice(mi * chunk // 2, (mi + 1) * chunk // 2)
            n = slice(ni * tb // 2, (ni + 1) * tb // 2)
            at_nxt[m, n] = jnp.dot(u_ref[m, :], ht_s[:, n], preferred_element_type=jnp.float32)

        def acc_piece(mi, ni):
            m = slice(mi * d // 2, (mi + 1) * d // 2)
            n = slice(ni * tb // 2, (ni + 1) * tb // 2)
            acc_s[m, n] += jnp.dot(vt_ref[m, :], coef_prv[:, n], preferred_element_type=jnp.float32)

        pieces = [functools.partial(f, mi, ni) for ni in range(2) for mi in range(2) for f in (at_piece, acc_piece)]
        _peer_gate_chunk(j, s1_s, s2_s, e1_s, e2_s, tau_s, rep_s, at_cur, coef_cur,
                         between=lambda r: pieces[r]())

    @pl.when(j % 2 == 0)
    def _even():
        step(at_a, at_b, coef_a, coef_b)

    @pl.when(j % 2 == 1)
    def _odd():
        step(at_b, at_a, coef_b, coef_a)

    @pl.when(j == n_chunks - 1)
    def _last():
        acc = acc_s[...] + jnp.dot(vtl_ref[...], coef_b[...], preferred_element_type=jnp.float32)
        o_ref[...] = x_ref[...] + acc.T


def _peer_residual(x, g, w_q, sub_keys, u_tab, v_tab, *, tokens=PEER_TOKENS, chunk=PEER_EXPERT_CHUNK):
    n, d = x.shape
    n_chunks = N_EXPERTS // chunk
    assert n % tokens == 0 and N_EXPERTS % chunk == 0 and chunk == SUBLANES * N_KEYS and n_chunks % 2 == 0
    wqt = w_q.T.astype(jnp.bfloat16)
    keys = sub_keys.astype(jnp.bfloat16)
    u_b = u_tab.astype(jnp.bfloat16)
    vt_b = v_tab.T.astype(jnp.bfloat16)
    f32 = jnp.float32
    once = pl.Buffered(1)
    per_head = pltpu.VMEM((PEER_HEADS, N_KEYS, tokens), f32)
    at_buf = pltpu.VMEM((chunk, tokens), f32)
    coef_buf = pltpu.VMEM((chunk, tokens), jnp.bfloat16)
    return pl.pallas_call(
        _peer_kernel,
        grid=(n // tokens, n_chunks),
        in_specs=[
            pl.BlockSpec((tokens, d), lambda i, j: (i, 0)),
            pl.BlockSpec((1, d), lambda i, j: (0, 0), pipeline_mode=once),
            pl.BlockSpec((PEER_HEADS * PEER_DKEY, d), lambda i, j: (0, 0), pipeline_mode=once),
            pl.BlockSpec((PEER_HEADS, 2, N_KEYS, PEER_HALF), lambda i, j: (0, 0, 0, 0), pipeline_mode=once),
            pl.BlockSpec((chunk, d), lambda i, j: (0, 0), pipeline_mode=once),
            pl.BlockSpec((chunk, d), lambda i, j: (jnp.minimum(j + 1, n_chunks - 1), 0)),
            pl.BlockSpec((d, chunk), lambda i, j: (0, jnp.maximum(j - 1, 0))),
            pl.BlockSpec((d, chunk), lambda i, j: (0, n_chunks - 1), pipeline_mode=once),
        ],
        out_specs=pl.BlockSpec((tokens, d), lambda i, j: (i, 0)),
        out_shape=jax.ShapeDtypeStruct((n, d), f32),
        scratch_shapes=[
            pltpu.VMEM((d, tokens), jnp.bfloat16),
            per_head, per_head, per_head, per_head,
            pltpu.VMEM((PEER_HEADS, SUBLANES, tokens), f32),
            pltpu.VMEM((2, PEER_TOPK, tokens), f32),
            pltpu.VMEM((2, PEER_HEADS, SUBLANES, SUBLANES, tokens), f32),
            at_buf, at_buf, coef_buf, coef_buf,
            pltpu.VMEM((d, tokens), f32),
        ],
        compiler_params=pltpu.CompilerParams(
            dimension_semantics=("arbitrary", "arbitrary"),
            vmem_limit_bytes=PEER_VMEM_LIMIT),
        name="peer",
    )(x, g.reshape(1, d), wqt, keys, u_b, u_b, vt_b, vt_b)


def _flash_kernel(q_ref, k_ref, v_ref, o_ref, *, scale, block):
    qi = pl.program_id(2)
    q = q_ref[0, 0]

    def kv_block(kb, carry, diagonal):
        m, l, acc = carry
        k0 = pl.multiple_of(kb * block, block)
        k = k_ref[0, 0, pl.ds(k0, block), :]
        v = v_ref[0, 0, pl.ds(k0, block), :]
        s = lax.dot_general(q, k, (((1,), (1,)), ((), ())), preferred_element_type=jnp.float32) * scale
        if diagonal:
            row = lax.broadcasted_iota(jnp.int32, s.shape, 0)
            col = lax.broadcasted_iota(jnp.int32, s.shape, 1)
            s = jnp.where(row >= col, s, NEG_INF)
        m_new = jnp.maximum(m, jnp.max(s, axis=1, keepdims=True))
        alpha = jnp.exp(m - m_new)
        p = jnp.exp(s - m_new)
        l_new = alpha * l + jnp.sum(p, axis=1, keepdims=True)
        acc_new = alpha * acc + jnp.dot(p.astype(v.dtype), v, preferred_element_type=jnp.float32)
        return m_new, l_new, acc_new

    init = (jnp.full((block, 1), NEG_INF, jnp.float32), jnp.zeros((block, 1), jnp.float32),
            jnp.zeros((block, v_ref.shape[3]), jnp.float32))
    carry = lax.fori_loop(0, qi, lambda kb, c: kv_block(kb, c, False), init)
    _, l, acc = kv_block(qi, carry, True)
    o_ref[0, 0] = acc / l


def _flash_attention(q, k, v, scale, block=FLASH_BLOCK):
    b, h, s, d_qk = q.shape
    d_v = v.shape[3]
    assert s % block == 0
    return pl.pallas_call(
        functools.partial(_flash_kernel, scale=scale, block=block),
        grid=(b, h, s // block),
        in_specs=[
            pl.BlockSpec((1, 1, block, d_qk), lambda bi, hi, qi: (bi, hi, qi, 0)),
            pl.BlockSpec((1, 1, s, d_qk), lambda bi, hi, qi: (bi, hi, 0, 0)),
            pl.BlockSpec((1, 1, s, d_v), lambda bi, hi, qi: (bi, hi, 0, 0)),
        ],
        out_specs=pl.BlockSpec((1, 1, block, d_v), lambda bi, hi, qi: (bi, hi, qi, 0)),
        out_shape=jax.ShapeDtypeStruct((b, h, s, d_v), jnp.float32),
        compiler_params=pltpu.CompilerParams(
            dimension_semantics=("arbitrary", "arbitrary", "arbitrary"),
            vmem_limit_bytes=FLASH_VMEM_LIMIT),
        name="mla_prompt_attention",
    )(q, k, v)


def _mlstm_kernel(q_ref, k_ref, v_ref, col_ref, row_ref, h_ref, c_ref, n_ref, m_ref):
    bsz, chunk, _ = q_ref.shape
    bf16 = jnp.bfloat16
    nt = (((1,), (1,)), ((), ()))

    @pl.when(pl.program_id(0) == 0)
    def _init():
        c_ref[...] = jnp.zeros_like(c_ref)
        n_ref[...] = jnp.zeros_like(n_ref)
        m_ref[...] = jnp.full_like(m_ref, M_INIT)

    t_idx = lax.broadcasted_iota(jnp.int32, (chunk, chunk), 0)
    s_idx = lax.broadcasted_iota(jnp.int32, (chunk, chunk), 1)
    causal = s_idx <= t_idx
    for b in range(bsz):
        for h in range(ML_HEADS):
            cols = slice(h * ML_DQK, (h + 1) * ML_DQK)
            qh, kh, vh = q_ref[b, :, cols], k_ref[b, :, cols], v_ref[b, :, cols]
            a_col = col_ref[b, :, h:h + 1]
            b_col = col_ref[b, :, ML_HEADS + h:ML_HEADS + h + 1]
            g_col = col_ref[b, :, 2 * ML_HEADS + h:2 * ML_HEADS + h + 1]
            a_row = row_ref[b, h:h + 1, :]
            m_prev = m_ref[b, h][:, 0:1]
            c_prev = c_ref[b, h]
            n_prev = n_ref[b, h]
            mg = jnp.maximum(m_prev, g_col)
            w = jnp.exp(jnp.where(causal, a_row - mg, NEG_INF))
            inter = jnp.exp(m_prev - mg)
            qb = qh.astype(bf16)
            vb = vh.astype(bf16)
            wqk = w * lax.dot_general(qb, kh.astype(bf16), nt, preferred_element_type=jnp.float32)
            num = (jnp.dot(wqk.astype(bf16), vb, preferred_element_type=jnp.float32)
                   + inter * jnp.dot(qb, c_prev.astype(bf16), preferred_element_type=jnp.float32))
            den = (jnp.sum(wqk, axis=1, keepdims=True)
                   + inter * jnp.sum(qh * n_prev, axis=1, keepdims=True))
            h_ref[b, :, cols] = num / jnp.maximum(jnp.abs(den), jnp.exp(-(b_col + mg)))
            b_last = b_col[chunk - 1:chunk, :]
            m_last = b_last + mg[chunk - 1:chunk, :]
            f_last = jnp.exp(m_prev + b_last - m_last)
            kw = jnp.exp(a_col + b_last - m_last) * kh
            c_ref[b, h] = f_last * c_prev + jnp.dot(kw.T.astype(bf16), vb, preferred_element_type=jnp.float32)
            n_ref[b, h] = f_last * n_prev + jnp.sum(kw, axis=0, keepdims=True)
            m_ref[b, h] = jnp.broadcast_to(m_last, m_ref.shape[2:])


def _mlstm_prompt(q, k, v, i_pre, f_pre):
    bsz, s, hd = q.shape
    chunk = ML_CHUNK
    nc = s // chunk
    assert s % chunk == 0
    f32 = jnp.float32
    lf = jax.nn.log_sigmoid(f_pre).reshape(bsz, nc, chunk, ML_HEADS)
    b_cum = jnp.cumsum(lf, axis=2)
    a = i_pre.reshape(bsz, nc, chunk, ML_HEADS) - b_cum
    g = lax.cummax(a, axis=2)
    col = jnp.concatenate([a, b_cum, g], axis=-1).reshape(bsz, s, 3 * ML_HEADS)
    row = jnp.transpose(a.reshape(bsz, s, ML_HEADS), (0, 2, 1))
    blk = pl.BlockSpec((bsz, chunk, hd), lambda c: (0, c, 0))
    state = lambda shape: pl.BlockSpec(shape, lambda c: (0,) * len(shape))
    c_shape = (bsz, ML_HEADS, ML_DQK, ML_DV)
    n_shape = (bsz, ML_HEADS, 1, ML_DQK)
    h, c_f, n_f, m_f = pl.pallas_call(
        _mlstm_kernel,
        grid=(nc,),
        in_specs=[blk, blk, blk,
                  pl.BlockSpec((bsz, chunk, 3 * ML_HEADS), lambda c: (0, c, 0)),
                  pl.BlockSpec((bsz, ML_HEADS, chunk), lambda c: (0, 0, c))],
        out_specs=[blk, state(c_shape), state(n_shape), state(n_shape)],
        out_shape=[jax.ShapeDtypeStruct((bsz, s, hd), f32), jax.ShapeDtypeStruct(c_shape, f32),
                   jax.ShapeDtypeStruct(n_shape, f32), jax.ShapeDtypeStruct(n_shape, f32)],
        compiler_params=pltpu.CompilerParams(dimension_semantics=("arbitrary",)),
        name="mlstm_prompt",
    )(q, k, v, col, row)
    return h.reshape(bsz, s, ML_HEADS, ML_DV), c_f, n_f[:, :, 0, :], m_f[:, :, 0, 0]


def _decode_kernel(pt_ref, ql_ref, qr_ref, cn_ref, kn_ref, *refs, scale, pages):
    c_refs = refs[:pages]
    k_refs = refs[pages:2 * pages]
    o_ref = refs[2 * pages]
    m_s, l_s, acc_s = refs[2 * pages + 1:]
    g = pl.program_id(1)
    bf16 = jnp.bfloat16
    nt = (((1,), (1,)), ((), ()))
    ql = ql_ref[0]
    qr = qr_ref[0]

    @pl.when(g == 0)
    def _init():
        m_s[...] = jnp.full_like(m_s, NEG_INF)
        l_s[...] = jnp.zeros_like(l_s)
        acc_s[...] = jnp.zeros_like(acc_s)

    def update(s, c):
        m_new = jnp.maximum(m_s[...], jnp.max(s, axis=1, keepdims=True))
        alpha = jnp.exp(m_s[...] - m_new)
        p = jnp.exp(s - m_new)
        l_s[...] = alpha * l_s[...] + jnp.sum(p, axis=1, keepdims=True)
        acc_s[...] = alpha * acc_s[...] + jnp.dot(p.astype(bf16), c, preferred_element_type=jnp.float32)
        m_s[...] = m_new

    c = jnp.concatenate([r[0] for r in c_refs], axis=0).astype(bf16)
    krt = jnp.concatenate([r[0] for r in k_refs], axis=1).astype(bf16)
    s = (lax.dot_general(ql, c, nt, preferred_element_type=jnp.float32)
         + jnp.dot(qr, krt, preferred_element_type=jnp.float32)) * scale
    update(s, c)

    @pl.when(g == pl.num_programs(1) - 1)
    def _finish():
        cn = cn_ref[0].astype(bf16)
        kn = kn_ref[0].astype(bf16)
        sn = (lax.dot_general(ql, cn, nt, preferred_element_type=jnp.float32)
              + lax.dot_general(qr, kn, nt, preferred_element_type=jnp.float32)) * scale
        t_of_row = lax.broadcasted_iota(jnp.int32, sn.shape, 0) // MLA_HEADS
        key = lax.broadcasted_iota(jnp.int32, sn.shape, 1)
        update(jnp.where(key <= t_of_row, sn, NEG_INF), cn)
        o_ref[0] = acc_s[...] / l_s[...]


def _mla_decode(q_lat, q_rope, c_new, kr_new, cache_latent, cache_rope, page_table, scale,
                pages=DECODE_PAGES):
    b, rows, _ = q_lat.shape
    t = c_new.shape[1]
    n_pages = page_table.shape[1]
    assert n_pages % pages == 0 and t <= DEC_PAD
    bf16 = jnp.bfloat16
    pad = ((0, 0), (0, DEC_PAD - t), (0, 0))

    def page_spec(shape, i):
        return pl.BlockSpec((1,) + shape, lambda bi, g, pt: (pt[bi, g * pages + i], 0, 0))

    rope_t = jnp.swapaxes(cache_rope, 1, 2)

    grid_spec = pltpu.PrefetchScalarGridSpec(
        num_scalar_prefetch=1,
        grid=(b, n_pages // pages),
        in_specs=[
            pl.BlockSpec((1, rows, KV_LORA), lambda bi, g, pt: (bi, 0, 0)),
            pl.BlockSpec((1, rows, QK_ROPE), lambda bi, g, pt: (bi, 0, 0)),
            pl.BlockSpec((1, DEC_PAD, KV_LORA), lambda bi, g, pt: (bi, 0, 0)),
            pl.BlockSpec((1, DEC_PAD, QK_ROPE), lambda bi, g, pt: (bi, 0, 0)),
        ] + [page_spec((PAGE_SIZE, KV_LORA), i) for i in range(pages)]
          + [page_spec((QK_ROPE, PAGE_SIZE), i) for i in range(pages)],
        out_specs=pl.BlockSpec((1, rows, KV_LORA), lambda bi, g, pt: (bi, 0, 0)),
        scratch_shapes=[pltpu.VMEM((rows, 1), jnp.float32), pltpu.VMEM((rows, 1), jnp.float32),
                        pltpu.VMEM((rows, KV_LORA), jnp.float32)],
    )
    return pl.pallas_call(
        functools.partial(_decode_kernel, scale=scale, pages=pages),
        grid_spec=grid_spec,
        out_shape=jax.ShapeDtypeStruct((b, rows, KV_LORA), jnp.float32),
        compiler_params=pltpu.CompilerParams(
            dimension_semantics=("arbitrary", "arbitrary"),
            vmem_limit_bytes=FLASH_VMEM_LIMIT),
        name="mla_decode_attention",
    )(page_table, q_lat.astype(bf16), q_rope.astype(bf16), jnp.pad(c_new, pad), jnp.pad(kr_new, pad),
      *([cache_latent] * pages), *([rope_t] * pages))


def _matmul_kernel(x_ref, w_ref, o_ref):
    o_ref[...] = jnp.dot(x_ref[...].astype(jnp.bfloat16), w_ref[...].astype(jnp.bfloat16),
                         preferred_element_type=jnp.float32)


def _project(x, w, block_m=MATMUL_BLOCK_M, block_n=MATMUL_BLOCK_N):
    lead, k = x.shape[:-1], x.shape[-1]
    n = w.shape[1]
    rows = x.reshape(-1, k)
    m = rows.shape[0]
    assert m % block_m == 0
    n_pad = -n % LANES
    if n_pad:
        w = jnp.pad(w, ((0, 0), (0, n_pad)))
    n_full = n + n_pad
    tn = max(t for t in range(LANES, min(block_n, n_full) + 1, LANES) if n_full % t == 0)
    out = pl.pallas_call(
        _matmul_kernel,
        grid=(m // block_m, n_full // tn),
        in_specs=[pl.BlockSpec((block_m, k), lambda i, j: (i, 0)),
                  pl.BlockSpec((k, tn), lambda i, j: (0, j))],
        out_specs=pl.BlockSpec((block_m, tn), lambda i, j: (i, j)),
        out_shape=jax.ShapeDtypeStruct((m, n_full), jnp.float32),
        compiler_params=pltpu.CompilerParams(
            dimension_semantics=("arbitrary", "arbitrary"),
            vmem_limit_bytes=FLASH_VMEM_LIMIT),
        name="projection",
    )(rows, w)
    return out[:, :n].reshape(lead + (n,))


def _rmsnorm(x, g):
    xf = x.astype(jnp.float32)
    y = xf * lax.rsqrt(jnp.mean(xf * xf, axis=-1, keepdims=True) + NORM_EPS)
    return (y * g.astype(jnp.float32)).astype(x.dtype)


def _layernorm(x, g, b):
    xf = x.astype(jnp.float32)
    mu = jnp.mean(xf, axis=-1, keepdims=True)
    var = jnp.mean(jnp.square(xf - mu), axis=-1, keepdims=True)
    y = (xf - mu) * lax.rsqrt(var + NORM_EPS)
    return (y * g.astype(jnp.float32) + b.astype(jnp.float32)).astype(x.dtype)


def _rope(x, pos):
    half = QK_ROPE // 2
    inv = ROPE_THETA ** (-jnp.arange(half, dtype=jnp.float32) / half)
    ang = pos.astype(jnp.float32)[:, None] * inv[None, :]
    ang = ang.reshape((ang.shape[0],) + (1,) * (x.ndim - 3) + (half,))
    cos, sin = jnp.cos(ang), jnp.sin(ang)
    xf = x.astype(jnp.float32)
    x1, x2 = xf[..., :half], xf[..., half:]
    return jnp.concatenate([x1 * cos - x2 * sin, x1 * sin + x2 * cos], axis=-1).astype(x.dtype)


def _even_project(h, pos, w_in, q_norm, w_uq, kv_norm, b_i, b_f):
    B, S, _ = h.shape
    f32 = jnp.float32
    z = _project(h, w_in)
    cq, ckv, mq, mk, mv, mi, mf, mo = jnp.split(z, IN_SPLITS, axis=-1)
    q = _project(_rmsnorm(cq, q_norm), w_uq).reshape(B, S, MLA_HEADS, QK_NOPE + QK_ROPE)
    q_nope = q[..., :QK_NOPE]
    q_rope = _rope(q[..., QK_NOPE:], pos)
    c_kv = _rmsnorm(ckv[..., :KV_LORA], kv_norm)
    k_rope = _rope(ckv[..., KV_LORA:], pos)
    mq = mq.reshape(B, S, ML_HEADS, ML_DQK).astype(f32)
    mk = mk.reshape(B, S, ML_HEADS, ML_DQK).astype(f32) * (ML_DQK ** -0.5)
    mv = mv.reshape(B, S, ML_HEADS, ML_DV).astype(f32)
    i_pre = mi.astype(f32) + b_i.astype(f32)
    f_pre = mf.astype(f32) + b_f.astype(f32)
    return q_nope, q_rope, c_kv, k_rope, mq, mk, mv, i_pre, f_pre, mo


def _mla_prompt(q_nope, q_rope, c_kv, k_rope, w_uk, w_uv):
    B, S = q_nope.shape[0], q_nope.shape[1]
    bf16 = jnp.bfloat16
    scale = (QK_NOPE + QK_ROPE) ** -0.5
    k_nope = jnp.einsum('bsc,chd->bshd', c_kv, w_uk)
    v = jnp.einsum('bsc,chd->bshd', c_kv, w_uv)
    q = jnp.concatenate([q_nope, q_rope], axis=-1)
    k = jnp.concatenate([k_nope, jnp.broadcast_to(k_rope[:, :, None, :], (B, S, MLA_HEADS, QK_ROPE))], axis=-1)
    heads_major = lambda a: jnp.transpose(a, (0, 2, 1, 3)).astype(bf16)
    o = _flash_attention(heads_major(q), heads_major(k), heads_major(v), scale)
    return jnp.transpose(o, (0, 2, 1, 3))


def _mla_sample(q_nope, q_rope, c_new, kr_new, cache_latent, cache_rope, page_table, w_uk, w_uv):
    B, T = q_nope.shape[0], q_nope.shape[1]
    scale = (QK_NOPE + QK_ROPE) ** -0.5
    q_lat = jnp.einsum('bthd,chd->bthc', q_nope, w_uk)
    o_lat = _mla_decode(q_lat.reshape(B, T * MLA_HEADS, KV_LORA), q_rope.reshape(B, T * MLA_HEADS, QK_ROPE),
                        c_new, kr_new, cache_latent, cache_rope, page_table, scale)
    return jnp.einsum('bthc,chd->bthd', o_lat.reshape(B, T, MLA_HEADS, KV_LORA), w_uv)


def _mlstm_chunked(q, k, v, i_pre, f_pre, C0, n0, m0):
    B, S, H, _ = q.shape
    L = ML_CHUNK if S % ML_CHUNK == 0 else S
    nc = S // L
    logf = jax.nn.log_sigmoid(f_pre)

    def chunks(a):
        return jnp.moveaxis(a.reshape((B, nc, L) + a.shape[2:]), 1, 0)

    causal = jnp.tril(jnp.ones((L, L), dtype=bool))[None, :, :, None]

    def step(carry, inp):
        C, n, m = carry
        qc, kc, vc, ic, lfc = inp
        b = jnp.cumsum(lfc, axis=1)
        a = ic - b
        m_t = b + jnp.maximum(m[:, None, :], lax.cummax(a, axis=1))
        d_log = (b - m_t)[:, :, None, :] + a[:, None, :, :]
        w = jnp.exp(jnp.where(causal, d_log, -jnp.inf))
        inter = jnp.exp(m[:, None, :] + b - m_t)
        wqk = w * jnp.einsum('bthd,bshd->btsh', qc, kc)
        num = (jnp.einsum('btsh,bshv->bthv', wqk, vc)
               + inter[..., None] * jnp.einsum('bthd,bhdv->bthv', qc, C))
        den = wqk.sum(axis=2) + inter * jnp.einsum('bthd,bhd->bth', qc, n)
        h = num / jnp.maximum(jnp.abs(den), jnp.exp(-m_t))[..., None]
        m_last = m_t[:, -1]
        w_s = jnp.exp(a + b[:, -1:, :] - m_last[:, None, :])
        f_last = jnp.exp(m + b[:, -1] - m_last)
        C_new = f_last[..., None, None] * C + jnp.einsum('bsh,bshd,bshv->bhdv', w_s, kc, vc)
        n_new = f_last[..., None] * n + jnp.einsum('bsh,bshd->bhd', w_s, kc)
        return (C_new, n_new, m_last), h

    (C_f, n_f, m_f), hs = lax.scan(step, (C0, n0, m0),
                                   (chunks(q), chunks(k), chunks(v), chunks(i_pre), chunks(logf)))
    h = jnp.moveaxis(hs, 0, 1).reshape(B, S, H, v.shape[-1])
    return h, C_f, n_f, m_f


def _even_merge(att, h_ml, o_pre, ml_norm, w_out):
    B, S = att.shape[0], att.shape[1]
    hn = h_ml * lax.rsqrt(jnp.mean(h_ml * h_ml, axis=-1, keepdims=True) + NORM_EPS)
    y_ml = (hn.reshape(B, S, ML_HEADS * ML_DV) * ml_norm.astype(jnp.float32)
            * jax.nn.sigmoid(o_pre.astype(jnp.float32))).astype(att.dtype)
    mix = jnp.concatenate([att.reshape(B, S, MLA_HEADS * V_HEAD), y_ml], axis=-1)
    return _project(mix, w_out)


def _conformer_conv(h, buf, w_pw1, b_pw1, w_dw, b_dw, ln_g, ln_b, w_pw2, b_pw2):
    a = _project(h, w_pw1) + b_pw1
    g = a[..., :D_MODEL] * jax.nn.sigmoid(a[..., D_MODEL:])
    full = jnp.concatenate([buf.astype(g.dtype), g], axis=1)
    y = lax.conv_general_dilated(full, w_dw[:, None, :].astype(full.dtype), window_strides=(1,),
                                 padding='VALID', dimension_numbers=('NWC', 'WIO', 'NWC'),
                                 feature_group_count=D_MODEL) + b_dw
    y = jax.nn.silu(_layernorm(y, ln_g, ln_b))
    return _project(y, w_pw2) + b_pw2, full[:, -(CONV_W - 1):]


def kernel(x_prompt, x_sample, cache_mla_latent, cache_mla_rope, state_mlstm_C, state_mlstm_n, state_mlstm_m, state_conv, page_table, attn_norm, w_in, q_norm, w_uq, kv_norm, w_uk, w_uv, b_igate, b_fgate, ml_norm, w_out, conv_norm, w_pw1, b_pw1, w_dw, b_dw, conv_ln_g, conv_ln_b, w_pw2, b_pw2, ffn_norm, peer_wq, peer_keys, peer_u, peer_v, final_norm):
    f32 = jnp.float32
    bp, sp = x_prompt.shape[0], x_prompt.shape[1]
    bs, ss = x_sample.shape[0], x_sample.shape[1]
    n_p, n_s = bp * sp, bs * ss
    pos_p = jnp.arange(sp, dtype=jnp.int32)
    pos_s = PAST_LEN + jnp.arange(ss, dtype=jnp.int32)
    past = page_table.shape[1] * PAGE_SIZE
    xp, xs = x_prompt, x_sample
    lat_p, rope_p, lat_s, rope_s = [], [], [], []
    cp_l, np_l, mp_l, cs_l, ns_l, ms_l = [], [], [], [], [], []
    convp_l, convs_l = [], []
    for layer in range(DEPTH):
        j = layer // 2
        if layer % 2 == 0:
            (qn_p, qr_p, c_p, kr_p, mq_p, mk_p, mv_p, ig_p, fg_p, og_p) = _even_project(
                _rmsnorm(xp, attn_norm[j]), pos_p, w_in[j], q_norm[j], w_uq[j], kv_norm[j], b_igate[j], b_fgate[j])
            (qn_s, qr_s, c_s, kr_s, mq_s, mk_s, mv_s, ig_s, fg_s, og_s) = _even_project(
                _rmsnorm(xs, attn_norm[j]), pos_s, w_in[j], q_norm[j], w_uq[j], kv_norm[j], b_igate[j], b_fgate[j])
            att_p = _mla_prompt(qn_p, qr_p, c_p, kr_p, w_uk[j], w_uv[j])
            att_s = _mla_sample(qn_s, qr_s, c_s, kr_s, cache_mla_latent[j], cache_mla_rope[j], page_table,
                                w_uk[j], w_uv[j])
            h_p, C_p, n_p_, m_p = _mlstm_prompt(mq_p.reshape(bp, sp, -1), mk_p.reshape(bp, sp, -1),
                                                mv_p.reshape(bp, sp, -1), ig_p, fg_p)
            h_s, C_s, n_s_, m_s = _mlstm_chunked(mq_s, mk_s, mv_s, ig_s, fg_s,
                                                 state_mlstm_C[j].astype(f32), state_mlstm_n[j].astype(f32),
                                                 state_mlstm_m[j].astype(f32))
            xp = xp + _even_merge(att_p, h_p, og_p, ml_norm[j], w_out[j])
            xs = xs + _even_merge(att_s, h_s, og_s, ml_norm[j], w_out[j])
            lat_p.append(c_p)
            rope_p.append(kr_p)
            lat_s.append(c_s)
            rope_s.append(kr_s)
            cp_l.append(C_p)
            np_l.append(n_p_)
            mp_l.append(m_p)
            cs_l.append(C_s)
            ns_l.append(n_s_)
            ms_l.append(m_s)
        else:
            buf0 = jnp.zeros((bp, CONV_W - 1, D_MODEL), xp.dtype)
            out_p, buf_p = _conformer_conv(_rmsnorm(xp, conv_norm[j]), buf0, w_pw1[j], b_pw1[j], w_dw[j], b_dw[j],
                                           conv_ln_g[j], conv_ln_b[j], w_pw2[j], b_pw2[j])
            out_s, buf_s = _conformer_conv(_rmsnorm(xs, conv_norm[j]), state_conv[j], w_pw1[j], b_pw1[j], w_dw[j],
                                           b_dw[j], conv_ln_g[j], conv_ln_b[j], w_pw2[j], b_pw2[j])
            xp = xp + out_p
            xs = xs + out_s
            convp_l.append(buf_p)
            convs_l.append(buf_s)
        rows = jnp.concatenate([xp.reshape(n_p, D_MODEL), xs.reshape(n_s, D_MODEL)], axis=0)
        rows = _peer_residual(rows, ffn_norm[layer], peer_wq[layer], peer_keys[layer],
                              peer_u[layer], peer_v[layer])
        xp = rows[:n_p].reshape(bp, sp, D_MODEL)
        xs = rows[n_p:].reshape(bs, ss, D_MODEL)
    y_prompt = _rmsnorm(xp, final_norm)
    y_sample = _rmsnorm(xs, final_norm)
    return (y_prompt, y_sample, jnp.stack(lat_p), jnp.stack(rope_p), jnp.stack(lat_s), jnp.stack(rope_s),
            jnp.stack(cp_l), jnp.stack(np_l), jnp.stack(mp_l), jnp.stack(cs_l), jnp.stack(ns_l), jnp.stack(ms_l),
            jnp.stack(convp_l), jnp.stack(convs_l))
```

```python
import functools
import math

import jax
import jax.numpy as jnp
import numpy as np
from jax import lax
from jax.experimental import pallas as pl
from jax.experimental.pallas import tpu as pltpu

D_MODEL = 1024
DEPTH = 2
PAST_LEN = 8192
PAGE_SIZE = 128
NORM_EPS = 1e-6

MLA_HEADS = 8
Q_LORA = 384
KV_LORA = 256
QK_NOPE = 64
QK_ROPE = 32
V_HEAD = 64
ROPE_THETA = 10000.0
Q_BLOCK = 128

ML_HEADS = 4
ML_DQK = 128
ML_DV = 128
ML_CHUNK = 128
M_INIT = -1e30

IN_WIDTHS = (Q_LORA, KV_LORA + QK_ROPE, ML_HEADS * ML_DQK, ML_HEADS * ML_DQK,
             ML_HEADS * ML_DV, ML_HEADS, ML_HEADS, ML_HEADS * ML_DV)
IN_SPLITS = tuple(int(c) for c in np.cumsum(IN_WIDTHS)[:-1])

CONV_W = 31

PEER_HEADS = 8
N_KEYS = 128
N_EXPERTS = N_KEYS * N_KEYS
PEER_DKEY = 256
PEER_HALF = PEER_DKEY // 2
PEER_TOPK = 16

SUBLANES = 8
LANES = 128
BF16_ROWS = 16
MATMUL_BLOCK_M = 512
MATMUL_BLOCK_N = 1408
PEER_TOKENS = 512
PEER_EXPERT_CHUNK = 1024
PEER_GATE_TILES = 4
PEER_SORT_LANES = 256
PEER_VMEM_LIMIT = 56 * 1024 * 1024
FLASH_BLOCK = 512
FLASH_HEADS = 2
DECODE_PAGES = 32
DEC_PAD = 16
FLASH_VMEM_LIMIT = 32 * 1024 * 1024

NEG_INF = float("-inf")
SQRT_HALF = math.sqrt(0.5)


def _peer_candidates(v1, v2):
    t = v1.shape[1]
    row = lax.broadcasted_iota(jnp.int32, (8, t), 0)
    slabs = [v1[0:8] + v2[0:1], v1[8:16] + v2[0:1], v1[0:8] + v2[1:2]]
    for b in range(2, 8):
        n_a = PEER_TOPK // (b + 1)
        slabs.append(jnp.where(row < n_a, v1[0:8] + v2[b:b + 1], NEG_INF))
    slabs.append(v1[0:1] + v2[8:16])
    return jnp.concatenate(slabs, axis=0)


def _extract_top(cur, n):
    vals = []
    for i in range(n):
        m = jnp.max(cur, axis=0, keepdims=True)
        vals.append(m)
        if i + 1 < n:
            cur = jnp.where(cur == m, NEG_INF, cur)
    return vals


def _merge_exchange_pairs(n):
    pairs = []
    p = 1
    while p < n:
        k = p
        while k >= 1:
            for j in range(k % p, n - k, 2 * k):
                for i in range(min(k, n - j - k)):
                    if (i + j) // (2 * p) == (i + j + k) // (2 * p):
                        pairs.append((i + j, i + j + k))
            k //= 2
        p *= 2
    return pairs


def _top_of_rows(s_ref, head, n, out_ref, lane_block):
    tiles = N_KEYS // SUBLANES
    for l0 in range(0, s_ref.shape[2], lane_block):
        lanes = slice(l0, l0 + lane_block)
        v = [s_ref[head, i * SUBLANES:(i + 1) * SUBLANES, lanes] for i in range(tiles)]
        for i, k in _merge_exchange_pairs(tiles):
            v[i], v[k] = jnp.maximum(v[i], v[k]), jnp.minimum(v[i], v[k])
        for r in range(n):
            m = jnp.max(v[0], axis=0, keepdims=True)
            out_ref[r:r + 1, lanes] = m
            if r + 1 < n:
                hit = v[0] == m
                for k in range(min(tiles, n) - 1 - r):
                    v[k] = jnp.where(hit, v[k + 1], v[k])


def _peer_prologue(x_ref, g_ref, wqt_ref, keys_ref, ht_s, s1_s, s2_s, e1_s, e2_s, tau_s, top_s):
    tb = x_ref.shape[0]
    x = x_ref[...]
    ms = jnp.mean(x * x, axis=-1, keepdims=True)
    hn = (x * lax.rsqrt(ms + NORM_EPS)) * g_ref[...]
    ht = hn.T.astype(jnp.bfloat16)
    ht_s[...] = ht
    qtb = jnp.dot(wqt_ref[...], ht, preferred_element_type=jnp.float32).astype(jnp.bfloat16)
    for h in range(PEER_HEADS):
        tops = []
        for p, s_ref in enumerate((s1_s, s2_s)):
            r0 = (h * 2 + p) * PEER_HALF
            s = jnp.dot(keys_ref[h, p], qtb[r0:r0 + PEER_HALF, :],
                        preferred_element_type=jnp.float32)
            s_ref[h] = s
            _top_of_rows(s_ref, h, PEER_TOPK, top_s.at[p], PEER_SORT_LANES)
            tops.append(top_s[p, 0:1, :])
        sc = _extract_top(_peer_candidates(top_s[0], top_s[1]), PEER_TOPK)
        z = jnp.zeros_like(sc[0])
        for k in range(PEER_TOPK):
            z = z + jnp.exp(sc[k] - sc[0])
        tau_s[h] = jnp.broadcast_to(sc[PEER_TOPK - 1], (SUBLANES, tb))
        e1_s[h] = jnp.exp(s1_s[h] - tops[0]) / z
        e2_s[h] = jnp.exp(s2_s[h] - tops[1])


def _peer_gate_chunk(j, s1_s, s2_s, e1_s, e2_s, tau_s, rep_s, at_ref, coef_ref, between=None):
    rows = at_ref.shape[0] // N_KEYS
    tb = at_ref.shape[1]
    e1_0 = pl.multiple_of(j * rows, SUBLANES)
    for h in range(PEER_HEADS):
        s1_blk = s1_s[h, pl.ds(e1_0, rows), :]
        e1_blk = e1_s[h, pl.ds(e1_0, rows), :]
        for r in range(rows):
            rep_s[0, h, r] = jnp.broadcast_to(s1_blk[r:r + 1, :], (SUBLANES, tb))
            rep_s[1, h, r] = jnp.broadcast_to(e1_blk[r:r + 1, :], (SUBLANES, tb))
    for r in range(rows):
        if between is not None:
            between(r)
        for t0 in range(0, N_KEYS // SUBLANES, PEER_GATE_TILES):
            gates = [None] * PEER_GATE_TILES
            for h in range(PEER_HEADS):
                s1r, e1r, tau = rep_s[0, h, r], rep_s[1, h, r], tau_s[h]
                for i in range(PEER_GATE_TILES):
                    r0 = (t0 + i) * SUBLANES
                    ssum = s1r + s2_s[h, r0:r0 + SUBLANES, :]
                    term = jnp.where(ssum >= tau, e1r * e2_s[h, r0:r0 + SUBLANES, :], 0.0)
                    gates[i] = term if gates[i] is None else gates[i] + term
            for i0 in range(0, PEER_GATE_TILES, BF16_ROWS // SUBLANES):
                halves = []
                for i in range(i0, i0 + BF16_ROWS // SUBLANES):
                    r0 = r * N_KEYS + (t0 + i) * SUBLANES
                    a = at_ref[r0:r0 + SUBLANES, :]
                    halves.append(gates[i] * ((0.5 * a) * (1.0 + lax.erf(a * SQRT_HALF))))
                row0 = r * N_KEYS + (t0 + i0) * SUBLANES
                coef_ref[row0:row0 + BF16_ROWS, :] = jnp.concatenate(halves, axis=0).astype(jnp.bfloat16)


def _peer_kernel(x_ref, g_ref, wqt_ref, keys_ref, u0_ref, u_ref, vt_ref, vtl_ref, o_ref,
                 ht_s, s1_s, s2_s, e1_s, e2_s, tau_s, top_s, rep_s,
                 at_a, at_b, coef_a, coef_b, acc_s):
    j = pl.program_id(1)
    n_chunks = pl.num_programs(1)

    @pl.when(j == 0)
    def _first():
        _peer_prologue(x_ref, g_ref, wqt_ref, keys_ref, ht_s, s1_s, s2_s, e1_s, e2_s, tau_s, top_s)
        at_a[...] = jnp.dot(u0_ref[...], ht_s[...], preferred_element_type=jnp.float32)
        coef_b[...] = jnp.zeros_like(coef_b)
        acc_s[...] = jnp.zeros_like(acc_s)

    def step(at_cur, at_nxt, coef_cur, coef_prv):
        chunk, tb = at_cur.shape
        d = acc_s.shape[0]

        def at_piece(mi, ni):
            m = slice(mi * chunk // 2, (mi + 1) * chunk // 2)
            n = slice(ni * tb // 2, (ni + 1) * tb // 2)
            at_nxt[m, n] = jnp.dot(u_ref[m, :], ht_s[:, n], preferred_element_type=jnp.float32)

        def acc_piece(mi, ni):
            m = slice(mi * d // 2, (mi + 1) * d // 2)
            n = slice(ni * tb // 2, (ni + 1) * tb // 2)
            acc_s[m, n] += jnp.dot(vt_ref[m, :], coef_prv[:, n], preferred_element_type=jnp.float32)

        pieces = [functools.partial(f, mi, ni) for ni in range(2) for mi in range(2) for f in (at_piece, acc_piece)]
        _peer_gate_chunk(j, s1_s, s2_s, e1_s, e2_s, tau_s, rep_s, at_cur, coef_cur,
                         between=lambda r: pieces[r]())

    @pl.when(j % 2 == 0)
    def _even():
        step(at_a, at_b, coef_a, coef_b)

    @pl.when(j % 2 == 1)
    def _odd():
        step(at_b, at_a, coef_b, coef_a)

    @pl.when(j == n_chunks - 1)
    def _last():
        acc = acc_s[...] + jnp.dot(vtl_ref[...], coef_b[...], preferred_element_type=jnp.float32)
        o_ref[...] = x_ref[...] + acc.T


def _peer_residual(x, g, w_q, sub_keys, u_tab, v_tab, *, tokens=PEER_TOKENS, chunk=PEER_EXPERT_CHUNK):
    n, d = x.shape
    n_chunks = N_EXPERTS // chunk
    assert n % tokens == 0 and N_EXPERTS % chunk == 0 and chunk == SUBLANES * N_KEYS and n_chunks % 2 == 0
    wqt = w_q.T.astype(jnp.bfloat16)
    keys = sub_keys.astype(jnp.bfloat16)
    u_b = u_tab.astype(jnp.bfloat16)
    vt_b = v_tab.T.astype(jnp.bfloat16)
    f32 = jnp.float32
    once = pl.Buffered(1)
    per_head = pltpu.VMEM((PEER_HEADS, N_KEYS, tokens), f32)
    at_buf = pltpu.VMEM((chunk, tokens), f32)
    coef_buf = pltpu.VMEM((chunk, tokens), jnp.bfloat16)
    return pl.pallas_call(
        _peer_kernel,
        grid=(n // tokens, n_chunks),
        in_specs=[
            pl.BlockSpec((tokens, d), lambda i, j: (i, 0)),
            pl.BlockSpec((1, d), lambda i, j: (0, 0), pipeline_mode=once),
            pl.BlockSpec((PEER_HEADS * PEER_DKEY, d), lambda i, j: (0, 0), pipeline_mode=once),
            pl.BlockSpec((PEER_HEADS, 2, N_KEYS, PEER_HALF), lambda i, j: (0, 0, 0, 0), pipeline_mode=once),
            pl.BlockSpec((chunk, d), lambda i, j: (0, 0), pipeline_mode=once),
            pl.BlockSpec((chunk, d), lambda i, j: (jnp.minimum(j + 1, n_chunks - 1), 0)),
            pl.BlockSpec((d, chunk), lambda i, j: (0, jnp.maximum(j - 1, 0))),
            pl.BlockSpec((d, chunk), lambda i, j: (0, n_chunks - 1), pipeline_mode=once),
        ],
        out_specs=pl.BlockSpec((tokens, d), lambda i, j: (i, 0)),
        out_shape=jax.ShapeDtypeStruct((n, d), f32),
        scratch_shapes=[
            pltpu.VMEM((d, tokens), jnp.bfloat16),
            per_head, per_head, per_head, per_head,
            pltpu.VMEM((PEER_HEADS, SUBLANES, tokens), f32),
            pltpu.VMEM((2, PEER_TOPK, tokens), f32),
            pltpu.VMEM((2, PEER_HEADS, SUBLANES, SUBLANES, tokens), f32),
            at_buf, at_buf, coef_buf, coef_buf,
            pltpu.VMEM((d, tokens), f32),
        ],
        compiler_params=pltpu.CompilerParams(
            dimension_semantics=("arbitrary", "arbitrary"),
            vmem_limit_bytes=PEER_VMEM_LIMIT),
        name="peer",
    )(x, g.reshape(1, d), wqt, keys, u_b, u_b, vt_b, vt_b)


def _flash_kernel(q_ref, k_ref, v_ref, o_ref, *, scale, block):
    qi = pl.program_id(2)
    heads = q_ref.shape[1]

    def kv_block(kb, carries, diagonal):
        return tuple(head_block(hh, kb, carries[hh], diagonal) for hh in range(heads))

    def head_block(hh, kb, carry, diagonal):
        m, l, acc = carry
        q = q_ref[0, hh]
        k0 = pl.multiple_of(kb * block, block)
        k = k_ref[0, hh, pl.ds(k0, block), :]
        v = v_ref[0, hh, pl.ds(k0, block), :]
        s = lax.dot_general(q, k, (((1,), (1,)), ((), ())), preferred_element_type=jnp.float32) * scale
        if diagonal:
            row = lax.broadcasted_iota(jnp.int32, s.shape, 0)
            col = lax.broadcasted_iota(jnp.int32, s.shape, 1)
            s = jnp.where(row >= col, s, NEG_INF)
        m_new = jnp.maximum(m, jnp.max(s, axis=1, keepdims=True))
        alpha = jnp.exp(m - m_new)
        p = jnp.exp(s - m_new)
        l_new = alpha * l + jnp.sum(p, axis=1, keepdims=True)
        acc_new = alpha * acc + jnp.dot(p.astype(v.dtype), v, preferred_element_type=jnp.float32)
        return m_new, l_new, acc_new

    init = tuple((jnp.full((block, 1), NEG_INF, jnp.float32), jnp.zeros((block, 1), jnp.float32),
                  jnp.zeros((block, v_ref.shape[3]), jnp.float32)) for _ in range(heads))
    carries = lax.fori_loop(0, qi, lambda kb, c: kv_block(kb, c, False), init)
    for hh, (_, l, acc) in enumerate(kv_block(qi, carries, True)):
        o_ref[0, hh] = acc / l


def _flash_attention(q, k, v, scale, block=FLASH_BLOCK, heads=FLASH_HEADS):
    b, h, s, d_qk = q.shape
    d_v = v.shape[3]
    assert s % block == 0 and h % heads == 0
    return pl.pallas_call(
        functools.partial(_flash_kernel, scale=scale, block=block),
        grid=(b, h // heads, s // block),
        in_specs=[
            pl.BlockSpec((1, heads, block, d_qk), lambda bi, hi, qi: (bi, hi, qi, 0)),
            pl.BlockSpec((1, heads, s, d_qk), lambda bi, hi, qi: (bi, hi, 0, 0)),
            pl.BlockSpec((1, heads, s, d_v), lambda bi, hi, qi: (bi, hi, 0, 0)),
        ],
        out_specs=pl.BlockSpec((1, heads, block, d_v), lambda bi, hi, qi: (bi, hi, qi, 0)),
        out_shape=jax.ShapeDtypeStruct((b, h, s, d_v), jnp.float32),
        compiler_params=pltpu.CompilerParams(
            dimension_semantics=("arbitrary", "arbitrary", "arbitrary"),
            vmem_limit_bytes=FLASH_VMEM_LIMIT),
        name="mla_prompt_attention",
    )(q, k, v)


def _mlstm_kernel(q_ref, k_ref, v_ref, col_ref, row_ref, h_ref, c_ref, n_ref, m_ref):
    bsz, chunk, _ = q_ref.shape
    bf16 = jnp.bfloat16
    nt = (((1,), (1,)), ((), ()))

    @pl.when(pl.program_id(0) == 0)
    def _init():
        c_ref[...] = jnp.zeros_like(c_ref)
        n_ref[...] = jnp.zeros_like(n_ref)
        m_ref[...] = jnp.full_like(m_ref, M_INIT)

    t_idx = lax.broadcasted_iota(jnp.int32, (chunk, chunk), 0)
    s_idx = lax.broadcasted_iota(jnp.int32, (chunk, chunk), 1)
    causal = s_idx <= t_idx
    for b in range(bsz):
        for h in range(ML_HEADS):
            cols = slice(h * ML_DQK, (h + 1) * ML_DQK)
            qh, kh, vh = q_ref[b, :, cols], k_ref[b, :, cols], v_ref[b, :, cols]
            a_col = col_ref[b, :, h:h + 1]
            b_col = col_ref[b, :, ML_HEADS + h:ML_HEADS + h + 1]
            g_col = col_ref[b, :, 2 * ML_HEADS + h:2 * ML_HEADS + h + 1]
            a_row = row_ref[b, h:h + 1, :]
            m_prev = m_ref[b, h][:, 0:1]
            c_prev = c_ref[b, h]
            n_prev = n_ref[b, h]
            mg = jnp.maximum(m_prev, g_col)
            w = jnp.exp(jnp.where(causal, a_row - mg, NEG_INF))
            inter = jnp.exp(m_prev - mg)
            qb = qh.astype(bf16)
            vb = vh.astype(bf16)
            wqk = w * lax.dot_general(qb, kh.astype(bf16), nt, preferred_element_type=jnp.float32)
            num = (jnp.dot(wqk.astype(bf16), vb, preferred_element_type=jnp.float32)
                   + inter * jnp.dot(qb, c_prev.astype(bf16), preferred_element_type=jnp.float32))
            den = (jnp.sum(wqk, axis=1, keepdims=True)
                   + inter * jnp.sum(qh * n_prev, axis=1, keepdims=True))
            h_ref[b, :, cols] = num / jnp.maximum(jnp.abs(den), jnp.exp(-(b_col + mg)))
            b_last = b_col[chunk - 1:chunk, :]
            m_last = b_last + mg[chunk - 1:chunk, :]
            f_last = jnp.exp(m_prev + b_last - m_last)
            kw = jnp.exp(a_col + b_last - m_last) * kh
            c_ref[b, h] = f_last * c_prev + jnp.dot(kw.T.astype(bf16), vb, preferred_element_type=jnp.float32)
            n_ref[b, h] = f_last * n_prev + jnp.sum(kw, axis=0, keepdims=True)
            m_ref[b, h] = jnp.broadcast_to(m_last, m_ref.shape[2:])


def _mlstm_prompt(q, k, v, i_pre, f_pre):
    bsz, s, hd = q.shape
    chunk = ML_CHUNK
    nc = s // chunk
    assert s % chunk == 0
    f32 = jnp.float32
    lf = jax.nn.log_sigmoid(f_pre).reshape(bsz, nc, chunk, ML_HEADS)
    b_cum = jnp.cumsum(lf, axis=2)
    a = i_pre.reshape(bsz, nc, chunk, ML_HEADS) - b_cum
    g = lax.cummax(a, axis=2)
    col = jnp.concatenate([a, b_cum, g], axis=-1).reshape(bsz, s, 3 * ML_HEADS)
    row = jnp.transpose(a.reshape(bsz, s, ML_HEADS), (0, 2, 1))
    blk = pl.BlockSpec((bsz, chunk, hd), lambda c: (0, c, 0))
    state = lambda shape: pl.BlockSpec(shape, lambda c: (0,) * len(shape))
    c_shape = (bsz, ML_HEADS, ML_DQK, ML_DV)
    n_shape = (bsz, ML_HEADS, 1, ML_DQK)
    h, c_f, n_f, m_f = pl.pallas_call(
        _mlstm_kernel,
        grid=(nc,),
        in_specs=[blk, blk, blk,
                  pl.BlockSpec((bsz, chunk, 3 * ML_HEADS), lambda c: (0, c, 0)),
                  pl.BlockSpec((bsz, ML_HEADS, chunk), lambda c: (0, 0, c))],
        out_specs=[blk, state(c_shape), state(n_shape), state(n_shape)],
        out_shape=[jax.ShapeDtypeStruct((bsz, s, hd), f32), jax.ShapeDtypeStruct(c_shape, f32),
                   jax.ShapeDtypeStruct(n_shape, f32), jax.ShapeDtypeStruct(n_shape, f32)],
        compiler_params=pltpu.CompilerParams(dimension_semantics=("arbitrary",)),
        name="mlstm_prompt",
    )(q, k, v, col, row)
    return h.reshape(bsz, s, ML_HEADS, ML_DV), c_f, n_f[:, :, 0, :], m_f[:, :, 0, 0]


def _decode_kernel(pt_ref, ql_ref, qr_ref, cn_ref, kn_ref, *refs, scale, pages):
    c_refs = refs[:pages]
    k_refs = refs[pages:2 * pages]
    o_ref = refs[2 * pages]
    m_s, l_s, acc_s = refs[2 * pages + 1:]
    g = pl.program_id(1)
    bf16 = jnp.bfloat16
    nt = (((1,), (1,)), ((), ()))
    ql = ql_ref[0]
    qr = qr_ref[0]

    @pl.when(g == 0)
    def _init():
        m_s[...] = jnp.full_like(m_s, NEG_INF)
        l_s[...] = jnp.zeros_like(l_s)
        acc_s[...] = jnp.zeros_like(acc_s)

    def update(s, c):
        m_new = jnp.maximum(m_s[...], jnp.max(s, axis=1, keepdims=True))
        alpha = jnp.exp(m_s[...] - m_new)
        p = jnp.exp(s - m_new)
        l_s[...] = alpha * l_s[...] + jnp.sum(p, axis=1, keepdims=True)
        acc_s[...] = alpha * acc_s[...] + jnp.dot(p.astype(bf16), c, preferred_element_type=jnp.float32)
        m_s[...] = m_new

    c = jnp.concatenate([r[0] for r in c_refs], axis=0).astype(bf16)
    krt = jnp.concatenate([r[0] for r in k_refs], axis=1).astype(bf16)
    s = (lax.dot_general(ql, c, nt, preferred_element_type=jnp.float32)
         + jnp.dot(qr, krt, preferred_element_type=jnp.float32)) * scale
    update(s, c)

    @pl.when(g == pl.num_programs(1) - 1)
    def _finish():
        cn = cn_ref[0].astype(bf16)
        kn = kn_ref[0].astype(bf16)
        sn = (lax.dot_general(ql, cn, nt, preferred_element_type=jnp.float32)
              + lax.dot_general(qr, kn, nt, preferred_element_type=jnp.float32)) * scale
        t_of_row = lax.broadcasted_iota(jnp.int32, sn.shape, 0) // MLA_HEADS
        key = lax.broadcasted_iota(jnp.int32, sn.shape, 1)
        update(jnp.where(key <= t_of_row, sn, NEG_INF), cn)
        o_ref[0] = acc_s[...] / l_s[...]


def _mla_decode(q_lat, q_rope, c_new, kr_new, cache_latent, cache_rope, page_table, scale,
                pages=DECODE_PAGES):
    b, rows, _ = q_lat.shape
    t = c_new.shape[1]
    n_pages = page_table.shape[1]
    assert n_pages % pages == 0 and t <= DEC_PAD
    bf16 = jnp.bfloat16
    pad = ((0, 0), (0, DEC_PAD - t), (0, 0))

    def page_spec(shape, i):
        return pl.BlockSpec((1,) + shape, lambda bi, g, pt: (pt[bi, g * pages + i], 0, 0))

    rope_t = jnp.swapaxes(cache_rope, 1, 2)

    grid_spec = pltpu.PrefetchScalarGridSpec(
        num_scalar_prefetch=1,
        grid=(b, n_pages // pages),
        in_specs=[
            pl.BlockSpec((1, rows, KV_LORA), lambda bi, g, pt: (bi, 0, 0)),
            pl.BlockSpec((1, rows, QK_ROPE), lambda bi, g, pt: (bi, 0, 0)),
            pl.BlockSpec((1, DEC_PAD, KV_LORA), lambda bi, g, pt: (bi, 0, 0)),
            pl.BlockSpec((1, DEC_PAD, QK_ROPE), lambda bi, g, pt: (bi, 0, 0)),
        ] + [page_spec((PAGE_SIZE, KV_LORA), i) for i in range(pages)]
          + [page_spec((QK_ROPE, PAGE_SIZE), i) for i in range(pages)],
        out_specs=pl.BlockSpec((1, rows, KV_LORA), lambda bi, g, pt: (bi, 0, 0)),
        scratch_shapes=[pltpu.VMEM((rows, 1), jnp.float32), pltpu.VMEM((rows, 1), jnp.float32),
                        pltpu.VMEM((rows, KV_LORA), jnp.float32)],
    )
    return pl.pallas_call(
        functools.partial(_decode_kernel, scale=scale, pages=pages),
        grid_spec=grid_spec,
        out_shape=jax.ShapeDtypeStruct((b, rows, KV_LORA), jnp.float32),
        compiler_params=pltpu.CompilerParams(
            dimension_semantics=("arbitrary", "arbitrary"),
            vmem_limit_bytes=FLASH_VMEM_LIMIT),
        name="mla_decode_attention",
    )(page_table, q_lat.astype(bf16), q_rope.astype(bf16), jnp.pad(c_new, pad), jnp.pad(kr_new, pad),
      *([cache_latent] * pages), *([rope_t] * pages))


def _matmul_kernel(x_ref, w_ref, o_ref):
    o_ref[...] = jnp.dot(x_ref[...].astype(jnp.bfloat16), w_ref[...].astype(jnp.bfloat16),
                         preferred_element_type=jnp.float32)


def _project(x, w, block_m=MATMUL_BLOCK_M, block_n=MATMUL_BLOCK_N):
    lead, k = x.shape[:-1], x.shape[-1]
    n = w.shape[1]
    rows = x.reshape(-1, k)
    m = rows.shape[0]
    assert m % block_m == 0
    n_pad = -n % LANES
    if n_pad:
        w = jnp.pad(w, ((0, 0), (0, n_pad)))
    n_full = n + n_pad
    tn = max(t for t in range(LANES, min(block_n, n_full) + 1, LANES) if n_full % t == 0)
    out = pl.pallas_call(
        _matmul_kernel,
        grid=(m // block_m, n_full // tn),
        in_specs=[pl.BlockSpec((block_m, k), lambda i, j: (i, 0)),
                  pl.BlockSpec((k, tn), lambda i, j: (0, j))],
        out_specs=pl.BlockSpec((block_m, tn), lambda i, j: (i, j)),
        out_shape=jax.ShapeDtypeStruct((m, n_full), jnp.float32),
        compiler_params=pltpu.CompilerParams(
            dimension_semantics=("arbitrary", "arbitrary"),
            vmem_limit_bytes=FLASH_VMEM_LIMIT),
        name="projection",
    )(rows, w)
    return out[:, :n].reshape(lead + (n,))


def _rmsnorm(x, g):
    xf = x.astype(jnp.float32)
    y = xf * lax.rsqrt(jnp.mean(xf * xf, axis=-1, keepdims=True) + NORM_EPS)
    return (y * g.astype(jnp.float32)).astype(x.dtype)


def _layernorm(x, g, b):
    xf = x.astype(jnp.float32)
    mu = jnp.mean(xf, axis=-1, keepdims=True)
    var = jnp.mean(jnp.square(xf - mu), axis=-1, keepdims=True)
    y = (xf - mu) * lax.rsqrt(var + NORM_EPS)
    return (y * g.astype(jnp.float32) + b.astype(jnp.float32)).astype(x.dtype)


def _rope(x, pos):
    half = QK_ROPE // 2
    inv = ROPE_THETA ** (-jnp.arange(half, dtype=jnp.float32) / half)
    ang = pos.astype(jnp.float32)[:, None] * inv[None, :]
    ang = ang.reshape((ang.shape[0],) + (1,) * (x.ndim - 3) + (half,))
    cos, sin = jnp.cos(ang), jnp.sin(ang)
    xf = x.astype(jnp.float32)
    x1, x2 = xf[..., :half], xf[..., half:]
    return jnp.concatenate([x1 * cos - x2 * sin, x1 * sin + x2 * cos], axis=-1).astype(x.dtype)


def _even_project(h, pos, w_in, q_norm, w_uq, kv_norm, b_i, b_f):
    B, S, _ = h.shape
    f32 = jnp.float32
    z = _project(h, w_in)
    cq, ckv, mq, mk, mv, mi, mf, mo = jnp.split(z, IN_SPLITS, axis=-1)
    q = _project(_rmsnorm(cq, q_norm), w_uq).reshape(B, S, MLA_HEADS, QK_NOPE + QK_ROPE)
    q_nope = q[..., :QK_NOPE]
    q_rope = _rope(q[..., QK_NOPE:], pos)
    c_kv = _rmsnorm(ckv[..., :KV_LORA], kv_norm)
    k_rope = _rope(ckv[..., KV_LORA:], pos)
    mq = mq.reshape(B, S, ML_HEADS, ML_DQK).astype(f32)
    mk = mk.reshape(B, S, ML_HEADS, ML_DQK).astype(f32) * (ML_DQK ** -0.5)
    mv = mv.reshape(B, S, ML_HEADS, ML_DV).astype(f32)
    i_pre = mi.astype(f32) + b_i.astype(f32)
    f_pre = mf.astype(f32) + b_f.astype(f32)
    return q_nope, q_rope, c_kv, k_rope, mq, mk, mv, i_pre, f_pre, mo


def _mla_prompt(q_nope, q_rope, c_kv, k_rope, w_uk, w_uv):
    B, S = q_nope.shape[0], q_nope.shape[1]
    bf16 = jnp.bfloat16
    scale = (QK_NOPE + QK_ROPE) ** -0.5
    k_nope = jnp.einsum('bsc,chd->bshd', c_kv, w_uk)
    v = jnp.einsum('bsc,chd->bshd', c_kv, w_uv)
    q = jnp.concatenate([q_nope, q_rope], axis=-1)
    k = jnp.concatenate([k_nope, jnp.broadcast_to(k_rope[:, :, None, :], (B, S, MLA_HEADS, QK_ROPE))], axis=-1)
    heads_major = lambda a: jnp.transpose(a, (0, 2, 1, 3)).astype(bf16)
    o = _flash_attention(heads_major(q), heads_major(k), heads_major(v), scale)
    return jnp.transpose(o, (0, 2, 1, 3))


def _mla_sample(q_nope, q_rope, c_new, kr_new, cache_latent, cache_rope, page_table, w_uk, w_uv):
    B, T = q_nope.shape[0], q_nope.shape[1]
    scale = (QK_NOPE + QK_ROPE) ** -0.5
    q_lat = jnp.einsum('bthd,chd->bthc', q_nope, w_uk)
    o_lat = _mla_decode(q_lat.reshape(B, T * MLA_HEADS, KV_LORA), q_rope.reshape(B, T * MLA_HEADS, QK_ROPE),
                        c_new, kr_new, cache_latent, cache_rope, page_table, scale)
    return jnp.einsum('bthc,chd->bthd', o_lat.reshape(B, T, MLA_HEADS, KV_LORA), w_uv)


def _mlstm_chunked(q, k, v, i_pre, f_pre, C0, n0, m0):
    B, S, H, _ = q.shape
    L = ML_CHUNK if S % ML_CHUNK == 0 else S
    nc = S // L
    logf = jax.nn.log_sigmoid(f_pre)

    def chunks(a):
        return jnp.moveaxis(a.reshape((B, nc, L) + a.shape[2:]), 1, 0)

    causal = jnp.tril(jnp.ones((L, L), dtype=bool))[None, :, :, None]

    def step(carry, inp):
        C, n, m = carry
        qc, kc, vc, ic, lfc = inp
        b = jnp.cumsum(lfc, axis=1)
        a = ic - b
        m_t = b + jnp.maximum(m[:, None, :], lax.cummax(a, axis=1))
        d_log = (b - m_t)[:, :, None, :] + a[:, None, :, :]
        w = jnp.exp(jnp.where(causal, d_log, -jnp.inf))
        inter = jnp.exp(m[:, None, :] + b - m_t)
        wqk = w * jnp.einsum('bthd,bshd->btsh', qc, kc)
        num = (jnp.einsum('btsh,bshv->bthv', wqk, vc)
               + inter[..., None] * jnp.einsum('bthd,bhdv->bthv', qc, C))
        den = wqk.sum(axis=2) + inter * jnp.einsum('bthd,bhd->bth', qc, n)
        h = num / jnp.maximum(jnp.abs(den), jnp.exp(-m_t))[..., None]
        m_last = m_t[:, -1]
        w_s = jnp.exp(a + b[:, -1:, :] - m_last[:, None, :])
        f_last = jnp.exp(m + b[:, -1] - m_last)
        C_new = f_last[..., None, None] * C + jnp.einsum('bsh,bshd,bshv->bhdv', w_s, kc, vc)
        n_new = f_last[..., None] * n + jnp.einsum('bsh,bshd->bhd', w_s, kc)
        return (C_new, n_new, m_last), h

    (C_f, n_f, m_f), hs = lax.scan(step, (C0, n0, m0),
                                   (chunks(q), chunks(k), chunks(v), chunks(i_pre), chunks(logf)))
    h = jnp.moveaxis(hs, 0, 1).reshape(B, S, H, v.shape[-1])
    return h, C_f, n_f, m_f


def _even_merge(att, h_ml, o_pre, ml_norm, w_out):
    B, S = att.shape[0], att.shape[1]
    hn = h_ml * lax.rsqrt(jnp.mean(h_ml * h_ml, axis=-1, keepdims=True) + NORM_EPS)
    y_ml = (hn.reshape(B, S, ML_HEADS * ML_DV) * ml_norm.astype(jnp.float32)
            * jax.nn.sigmoid(o_pre.astype(jnp.float32))).astype(att.dtype)
    mix = jnp.concatenate([att.reshape(B, S, MLA_HEADS * V_HEAD), y_ml], axis=-1)
    return _project(mix, w_out)


def _conformer_conv(h, buf, w_pw1, b_pw1, w_dw, b_dw, ln_g, ln_b, w_pw2, b_pw2):
    a = _project(h, w_pw1) + b_pw1
    g = a[..., :D_MODEL] * jax.nn.sigmoid(a[..., D_MODEL:])
    full = jnp.concatenate([buf.astype(g.dtype), g], axis=1)
    y = lax.conv_general_dilated(full, w_dw[:, None, :].astype(full.dtype), window_strides=(1,),
                                 padding='VALID', dimension_numbers=('NWC', 'WIO', 'NWC'),
                                 feature_group_count=D_MODEL) + b_dw
    y = jax.nn.silu(_layernorm(y, ln_g, ln_b))
    return _project(y, w_pw2) + b_pw2, full[:, -(CONV_W - 1):]


def kernel(x_prompt, x_sample, cache_mla_latent, cache_mla_rope, state_mlstm_C, state_mlstm_n, state_mlstm_m, state_conv, page_table, attn_norm, w_in, q_norm, w_uq, kv_norm, w_uk, w_uv, b_igate, b_fgate, ml_norm, w_out, conv_norm, w_pw1, b_pw1, w_dw, b_dw, conv_ln_g, conv_ln_b, w_pw2, b_pw2, ffn_norm, peer_wq, peer_keys, peer_u, peer_v, final_norm):
    f32 = jnp.float32
    bp, sp = x_prompt.shape[0], x_prompt.shape[1]
    bs, ss = x_sample.shape[0], x_sample.shape[1]
    n_p, n_s = bp * sp, bs * ss
    pos_p = jnp.arange(sp, dtype=jnp.int32)
    pos_s = PAST_LEN + jnp.arange(ss, dtype=jnp.int32)
    past = page_table.shape[1] * PAGE_SIZE
    xp, xs = x_prompt, x_sample
    lat_p, rope_p, lat_s, rope_s = [], [], [], []
    cp_l, np_l, mp_l, cs_l, ns_l, ms_l = [], [], [], [], [], []
    convp_l, convs_l = [], []
    for layer in range(DEPTH):
        j = layer // 2
        if layer % 2 == 0:
            (qn_p, qr_p, c_p, kr_p, mq_p, mk_p, mv_p, ig_p, fg_p, og_p) = _even_project(
                _rmsnorm(xp, attn_norm[j]), pos_p, w_in[j], q_norm[j], w_uq[j], kv_norm[j], b_igate[j], b_fgate[j])
            (qn_s, qr_s, c_s, kr_s, mq_s, mk_s, mv_s, ig_s, fg_s, og_s) = _even_project(
                _rmsnorm(xs, attn_norm[j]), pos_s, w_in[j], q_norm[j], w_uq[j], kv_norm[j], b_igate[j], b_fgate[j])
            att_p = _mla_prompt(qn_p, qr_p, c_p, kr_p, w_uk[j], w_uv[j])
            att_s = _mla_sample(qn_s, qr_s, c_s, kr_s, cache_mla_latent[j], cache_mla_rope[j], page_table,
                                w_uk[j], w_uv[j])
            h_p, C_p, n_p_, m_p = _mlstm_prompt(mq_p.reshape(bp, sp, -1), mk_p.reshape(bp, sp, -1),
                                                mv_p.reshape(bp, sp, -1), ig_p, fg_p)
            h_s, C_s, n_s_, m_s = _mlstm_chunked(mq_s, mk_s, mv_s, ig_s, fg_s,
                                                 state_mlstm_C[j].astype(f32), state_mlstm_n[j].astype(f32),
                                                 state_mlstm_m[j].astype(f32))
            xp = xp + _even_merge(att_p, h_p, og_p, ml_norm[j], w_out[j])
            xs = xs + _even_merge(att_s, h_s, og_s, ml_norm[j], w_out[j])
            lat_p.append(c_p)
            rope_p.append(kr_p)
            lat_s.append(c_s)
            rope_s.append(kr_s)
            cp_l.append(C_p)
            np_l.append(n_p_)
            mp_l.append(m_p)
            cs_l.append(C_s)
            ns_l.append(n_s_)
            ms_l.append(m_s)
        else:
            buf0 = jnp.zeros((bp, CONV_W - 1, D_MODEL), xp.dtype)
            out_p, buf_p = _conformer_conv(_rmsnorm(xp, conv_norm[j]), buf0, w_pw1[j], b_pw1[j], w_dw[j], b_dw[j],
                                           conv_ln_g[j], conv_ln_b[j], w_pw2[j], b_pw2[j])
            out_s, buf_s = _conformer_conv(_rmsnorm(xs, conv_norm[j]), state_conv[j], w_pw1[j], b_pw1[j], w_dw[j],
                                           b_dw[j], conv_ln_g[j], conv_ln_b[j], w_pw2[j], b_pw2[j])
            xp = xp + out_p
            xs = xs + out_s
            convp_l.append(buf_p)
            convs_l.append(buf_s)
        rows = jnp.concatenate([xp.reshape(n_p, D_MODEL), xs.reshape(n_s, D_MODEL)], axis=0)
        rows = _peer_residual(rows, ffn_norm[layer], peer_wq[layer], peer_keys[layer],
                              peer_u[layer], peer_v[layer])
        xp = rows[:n_p].reshape(bp, sp, D_MODEL)
        xs = rows[n_p:].reshape(bs, ss, D_MODEL)
    y_prompt = _rmsnorm(xp, final_norm)
    y_sample = _rmsnorm(xs, final_norm)
    return (y_prompt, y_sample, jnp.stack(lat_p), jnp.stack(rope_p), jnp.stack(lat_s), jnp.stack(rope_s),
            jnp.stack(cp_l), jnp.stack(np_l), jnp.stack(mp_l), jnp.stack(cs_l), jnp.stack(ns_l), jnp.stack(ms_l),
            jnp.stack(convp_l), jnp.stack(convs_l))
```

```python
import functools
import math

import jax
import jax.numpy as jnp
import numpy as np
from jax import lax
from jax.experimental import pallas as pl
from jax.experimental.pallas import tpu as pltpu

D_MODEL = 1024
DEPTH = 2
PAST_LEN = 8192
PAGE_SIZE = 128
NORM_EPS = 1e-6

MLA_HEADS = 8
Q_LORA = 384
KV_LORA = 256
QK_NOPE = 64
QK_ROPE = 32
V_HEAD = 64
ROPE_THETA = 10000.0
Q_BLOCK = 128

ML_HEADS = 4
ML_DQK = 128
ML_DV = 128
ML_CHUNK = 128
M_INIT = -1e30

IN_WIDTHS = (Q_LORA, KV_LORA + QK_ROPE, ML_HEADS * ML_DQK, ML_HEADS * ML_DQK,
             ML_HEADS * ML_DV, ML_HEADS, ML_HEADS, ML_HEADS * ML_DV)
IN_SPLITS = tuple(int(c) for c in np.cumsum(IN_WIDTHS)[:-1])

CONV_W = 31

PEER_HEADS = 8
N_KEYS = 128
N_EXPERTS = N_KEYS * N_KEYS
PEER_DKEY = 256
PEER_HALF = PEER_DKEY // 2
PEER_TOPK = 16

SUBLANES = 8
LANES = 128
BF16_ROWS = 16
MATMUL_BLOCK_M = 512
MATMUL_BLOCK_N = 1408
PEER_TOKENS = 512
PEER_EXPERT_CHUNK = 1024
PEER_GATE_TILES = 4
PEER_SORT_LANES = 256
PEER_VMEM_LIMIT = 56 * 1024 * 1024
FLASH_BLOCK = 512
FLASH_HEADS = 2
DECODE_PAGES = 32
DEC_PAD = 16
FLASH_VMEM_LIMIT = 32 * 1024 * 1024

NEG_INF = float("-inf")
SQRT_HALF = math.sqrt(0.5)


def _peer_candidates(v1, v2):
    t = v1.shape[1]
    row = lax.broadcasted_iota(jnp.int32, (8, t), 0)
    slabs = [v1[0:8] + v2[0:1], v1[8:16] + v2[0:1], v1[0:8] + v2[1:2]]
    for b in range(2, 8):
        n_a = PEER_TOPK // (b + 1)
        slabs.append(jnp.where(row < n_a, v1[0:8] + v2[b:b + 1], NEG_INF))
    slabs.append(v1[0:1] + v2[8:16])
    return jnp.concatenate(slabs, axis=0)


def _extract_top(cur, n):
    vals = []
    for i in range(n):
        m = jnp.max(cur, axis=0, keepdims=True)
        vals.append(m)
        if i + 1 < n:
            cur = jnp.where(cur == m, NEG_INF, cur)
    return vals


def _merge_exchange_pairs(n):
    pairs = []
    p = 1
    while p < n:
        k = p
        while k >= 1:
            for j in range(k % p, n - k, 2 * k):
                for i in range(min(k, n - j - k)):
                    if (i + j) // (2 * p) == (i + j + k) // (2 * p):
                        pairs.append((i + j, i + j + k))
            k //= 2
        p *= 2
    return pairs


def _top_of_rows(s_ref, head, n, out_ref, lane_block):
    tiles = N_KEYS // SUBLANES
    for l0 in range(0, s_ref.shape[2], lane_block):
        lanes = slice(l0, l0 + lane_block)
        v = [s_ref[head, i * SUBLANES:(i + 1) * SUBLANES, lanes] for i in range(tiles)]
        for i, k in _merge_exchange_pairs(tiles):
            v[i], v[k] = jnp.maximum(v[i], v[k]), jnp.minimum(v[i], v[k])
        for r in range(n):
            m = jnp.max(v[0], axis=0, keepdims=True)
            out_ref[r:r + 1, lanes] = m
            if r + 1 < n:
                hit = v[0] == m
                for k in range(min(tiles, n) - 1 - r):
                    v[k] = jnp.where(hit, v[k + 1], v[k])


def _peer_prologue(x_ref, g_ref, wqt_ref, keys_ref, ht_s, s1_s, s2_s, e1_s, e2_s, tau_s, top_s):
    tb = x_ref.shape[0]
    x = x_ref[...]
    ms = jnp.mean(x * x, axis=-1, keepdims=True)
    hn = (x * lax.rsqrt(ms + NORM_EPS)) * g_ref[...]
    ht = hn.T.astype(jnp.bfloat16)
    ht_s[...] = ht
    qtb = jnp.dot(wqt_ref[...], ht, preferred_element_type=jnp.float32).astype(jnp.bfloat16)
    for h in range(PEER_HEADS):
        tops = []
        for p, s_ref in enumerate((s1_s, s2_s)):
            r0 = (h * 2 + p) * PEER_HALF
            s = jnp.dot(keys_ref[h, p], qtb[r0:r0 + PEER_HALF, :],
                        preferred_element_type=jnp.float32)
            s_ref[h] = s
            _top_of_rows(s_ref, h, PEER_TOPK, top_s.at[p], PEER_SORT_LANES)
            tops.append(top_s[p, 0:1, :])
        sc = _extract_top(_peer_candidates(top_s[0], top_s[1]), PEER_TOPK)
        z = jnp.zeros_like(sc[0])
        for k in range(PEER_TOPK):
            z = z + jnp.exp(sc[k] - sc[0])
        tau_s[h] = jnp.broadcast_to(sc[PEER_TOPK - 1], (SUBLANES, tb))
        e1_s[h] = jnp.exp(s1_s[h] - tops[0]) / z
        e2_s[h] = jnp.exp(s2_s[h] - tops[1])


def _peer_gate_chunk(j, s1_s, s2_s, e1_s, e2_s, tau_s, rep_s, at_ref, coef_ref, between=None):
    rows = at_ref.shape[0] // N_KEYS
    tb = at_ref.shape[1]
    e1_0 = pl.multiple_of(j * rows, SUBLANES)
    for h in range(PEER_HEADS):
        s1_blk = s1_s[h, pl.ds(e1_0, rows), :]
        e1_blk = e1_s[h, pl.ds(e1_0, rows), :]
        for r in range(rows):
            rep_s[0, h, r] = jnp.broadcast_to(s1_blk[r:r + 1, :], (SUBLANES, tb))
            rep_s[1, h, r] = jnp.broadcast_to(e1_blk[r:r + 1, :], (SUBLANES, tb))
    for r in range(rows):
        if between is not None:
            between(r)
        for t0 in range(0, N_KEYS // SUBLANES, PEER_GATE_TILES):
            gates = [None] * PEER_GATE_TILES
            for h in range(PEER_HEADS):
                s1r, e1r, tau = rep_s[0, h, r], rep_s[1, h, r], tau_s[h]
                for i in range(PEER_GATE_TILES):
                    r0 = (t0 + i) * SUBLANES
                    ssum = s1r + s2_s[h, r0:r0 + SUBLANES, :]
                    term = jnp.where(ssum >= tau, e1r * e2_s[h, r0:r0 + SUBLANES, :], 0.0)
                    gates[i] = term if gates[i] is None else gates[i] + term
            for i0 in range(0, PEER_GATE_TILES, BF16_ROWS // SUBLANES):
                halves = []
                for i in range(i0, i0 + BF16_ROWS // SUBLANES):
                    r0 = r * N_KEYS + (t0 + i) * SUBLANES
                    a = at_ref[r0:r0 + SUBLANES, :]
                    halves.append(gates[i] * ((0.5 * a) * (1.0 + lax.erf(a * SQRT_HALF))))
                row0 = r * N_KEYS + (t0 + i0) * SUBLANES
                coef_ref[row0:row0 + BF16_ROWS, :] = jnp.concatenate(halves, axis=0).astype(jnp.bfloat16)


def _peer_kernel(x_ref, g_ref, wqt_ref, keys_ref, u0_ref, u_ref, vt_ref, vtl_ref, o_ref,
                 ht_s, s1_s, s2_s, e1_s, e2_s, tau_s, top_s, rep_s,
                 at_a, at_b, coef_a, coef_b, acc_s):
    j = pl.program_id(1)
    n_chunks = pl.num_programs(1)

    @pl.when(j == 0)
    def _first():
        _peer_prologue(x_ref, g_ref, wqt_ref, keys_ref, ht_s, s1_s, s2_s, e1_s, e2_s, tau_s, top_s)
        at_a[...] = jnp.dot(u0_ref[...], ht_s[...], preferred_element_type=jnp.float32)
        coef_b[...] = jnp.zeros_like(coef_b)
        acc_s[...] = jnp.zeros_like(acc_s)

    def step(at_cur, at_nxt, coef_cur, coef_prv):
        chunk, tb = at_cur.shape
        d = acc_s.shape[0]

        def at_piece(mi, ni):
            m = slice(mi * chunk // 2, (mi + 1) * chunk // 2)
            n = slice(ni * tb // 2, (ni + 1) * tb // 2)
            at_nxt[m, n] = jnp.dot(u_ref[m, :], ht_s[:, n], preferred_element_type=jnp.float32)

        def acc_piece(mi, ni):
            m = slice(mi * d // 2, (mi + 1) * d // 2)
            n = slice(ni * tb // 2, (ni + 1) * tb // 2)
            acc_s[m, n] += jnp.dot(vt_ref[m, :], coef_prv[:, n], preferred_element_type=jnp.float32)

        pieces = [functools.partial(f, mi, ni) for ni in range(2) for mi in range(2) for f in (at_piece, acc_piece)]
        _peer_gate_chunk(j, s1_s, s2_s, e1_s, e2_s, tau_s, rep_s, at_cur, coef_cur,
                         between=lambda r: pieces[r]())

    @pl.when(j % 2 == 0)
    def _even():
        step(at_a, at_b, coef_a, coef_b)

    @pl.when(j % 2 == 1)
    def _odd():
        step(at_b, at_a, coef_b, coef_a)

    @pl.when(j == n_chunks - 1)
    def _last():
        acc = acc_s[...] + jnp.dot(vtl_ref[...], coef_b[...], preferred_element_type=jnp.float32)
        o_ref[...] = x_ref[...] + acc.T


def _peer_tables(w_q, sub_keys, u_tab, v_tab):
    bf16 = jnp.bfloat16
    return w_q.T.astype(bf16), sub_keys.astype(bf16), u_tab.astype(bf16), v_tab.T.astype(bf16)


def _peer_residual(x, g, tables, *, tokens=PEER_TOKENS, chunk=PEER_EXPERT_CHUNK):
    n, d = x.shape
    n_chunks = N_EXPERTS // chunk
    assert n % tokens == 0 and N_EXPERTS % chunk == 0 and chunk == SUBLANES * N_KEYS and n_chunks % 2 == 0
    wqt, keys, u_b, vt_b = tables
    f32 = jnp.float32
    once = pl.Buffered(1)
    per_head = pltpu.VMEM((PEER_HEADS, N_KEYS, tokens), f32)
    at_buf = pltpu.VMEM((chunk, tokens), f32)
    coef_buf = pltpu.VMEM((chunk, tokens), jnp.bfloat16)
    return pl.pallas_call(
        _peer_kernel,
        grid=(n // tokens, n_chunks),
        in_specs=[
            pl.BlockSpec((tokens, d), lambda i, j: (i, 0)),
            pl.BlockSpec((1, d), lambda i, j: (0, 0), pipeline_mode=once),
            pl.BlockSpec((PEER_HEADS * PEER_DKEY, d), lambda i, j: (0, 0), pipeline_mode=once),
            pl.BlockSpec((PEER_HEADS, 2, N_KEYS, PEER_HALF), lambda i, j: (0, 0, 0, 0), pipeline_mode=once),
            pl.BlockSpec((chunk, d), lambda i, j: (0, 0), pipeline_mode=once),
            pl.BlockSpec((chunk, d), lambda i, j: (jnp.minimum(j + 1, n_chunks - 1), 0)),
            pl.BlockSpec((d, chunk), lambda i, j: (0, jnp.maximum(j - 1, 0))),
            pl.BlockSpec((d, chunk), lambda i, j: (0, n_chunks - 1), pipeline_mode=once),
        ],
        out_specs=pl.BlockSpec((tokens, d), lambda i, j: (i, 0)),
        out_shape=jax.ShapeDtypeStruct((n, d), f32),
        scratch_shapes=[
            pltpu.VMEM((d, tokens), jnp.bfloat16),
            per_head, per_head, per_head, per_head,
            pltpu.VMEM((PEER_HEADS, SUBLANES, tokens), f32),
            pltpu.VMEM((2, PEER_TOPK, tokens), f32),
            pltpu.VMEM((2, PEER_HEADS, SUBLANES, SUBLANES, tokens), f32),
            at_buf, at_buf, coef_buf, coef_buf,
            pltpu.VMEM((d, tokens), f32),
        ],
        compiler_params=pltpu.CompilerParams(
            dimension_semantics=("arbitrary", "arbitrary"),
            vmem_limit_bytes=PEER_VMEM_LIMIT),
        name="peer",
    )(x, g.reshape(1, d), wqt, keys, u_b, u_b, vt_b, vt_b)


def _flash_kernel(q_ref, k_ref, v_ref, o_ref, *, scale, block):
    qi = pl.program_id(2)
    heads = q_ref.shape[1]

    def kv_block(kb, carries, diagonal):
        return tuple(head_block(hh, kb, carries[hh], diagonal) for hh in range(heads))

    def head_block(hh, kb, carry, diagonal):
        m, l, acc = carry
        q = q_ref[0, hh]
        k0 = pl.multiple_of(kb * block, block)
        k = k_ref[0, hh, pl.ds(k0, block), :]
        v = v_ref[0, hh, pl.ds(k0, block), :]
        s = lax.dot_general(q, k, (((1,), (1,)), ((), ())), preferred_element_type=jnp.float32) * scale
        if diagonal:
            row = lax.broadcasted_iota(jnp.int32, s.shape, 0)
            col = lax.broadcasted_iota(jnp.int32, s.shape, 1)
            s = jnp.where(row >= col, s, NEG_INF)
        m_new = jnp.maximum(m, jnp.max(s, axis=1, keepdims=True))
        alpha = jnp.exp(m - m_new)
        p = jnp.exp(s - m_new)
        l_new = alpha * l + jnp.sum(p, axis=1, keepdims=True)
        acc_new = alpha * acc + jnp.dot(p.astype(v.dtype), v, preferred_element_type=jnp.float32)
        return m_new, l_new, acc_new

    init = tuple((jnp.full((block, 1), NEG_INF, jnp.float32), jnp.zeros((block, 1), jnp.float32),
                  jnp.zeros((block, v_ref.shape[3]), jnp.float32)) for _ in range(heads))
    carries = lax.fori_loop(0, qi, lambda kb, c: kv_block(kb, c, False), init)
    for hh, (_, l, acc) in enumerate(kv_block(qi, carries, True)):
        o_ref[0, hh] = acc / l


def _flash_attention(q, k, v, scale, block=FLASH_BLOCK, heads=FLASH_HEADS):
    b, h, s, d_qk = q.shape
    d_v = v.shape[3]
    assert s % block == 0 and h % heads == 0
    return pl.pallas_call(
        functools.partial(_flash_kernel, scale=scale, block=block),
        grid=(b, h // heads, s // block),
        in_specs=[
            pl.BlockSpec((1, heads, block, d_qk), lambda bi, hi, qi: (bi, hi, qi, 0)),
            pl.BlockSpec((1, heads, s, d_qk), lambda bi, hi, qi: (bi, hi, 0, 0)),
            pl.BlockSpec((1, heads, s, d_v), lambda bi, hi, qi: (bi, hi, 0, 0)),
        ],
        out_specs=pl.BlockSpec((1, heads, block, d_v), lambda bi, hi, qi: (bi, hi, qi, 0)),
        out_shape=jax.ShapeDtypeStruct((b, h, s, d_v), jnp.float32),
        compiler_params=pltpu.CompilerParams(
            dimension_semantics=("arbitrary", "arbitrary", "arbitrary"),
            vmem_limit_bytes=FLASH_VMEM_LIMIT),
        name="mla_prompt_attention",
    )(q, k, v)


def _mlstm_kernel(q_ref, k_ref, v_ref, col_ref, row_ref, h_ref, c_ref, n_ref, m_ref):
    bsz, chunk, _ = q_ref.shape
    bf16 = jnp.bfloat16
    nt = (((1,), (1,)), ((), ()))

    @pl.when(pl.program_id(0) == 0)
    def _init():
        c_ref[...] = jnp.zeros_like(c_ref)
        n_ref[...] = jnp.zeros_like(n_ref)
        m_ref[...] = jnp.full_like(m_ref, M_INIT)

    t_idx = lax.broadcasted_iota(jnp.int32, (chunk, chunk), 0)
    s_idx = lax.broadcasted_iota(jnp.int32, (chunk, chunk), 1)
    causal = s_idx <= t_idx
    for b in range(bsz):
        for h in range(ML_HEADS):
            cols = slice(h * ML_DQK, (h + 1) * ML_DQK)
            qh, kh, vh = q_ref[b, :, cols], k_ref[b, :, cols], v_ref[b, :, cols]
            a_col = col_ref[b, :, h:h + 1]
            b_col = col_ref[b, :, ML_HEADS + h:ML_HEADS + h + 1]
            g_col = col_ref[b, :, 2 * ML_HEADS + h:2 * ML_HEADS + h + 1]
            a_row = row_ref[b, h:h + 1, :]
            m_prev = m_ref[b, h][:, 0:1]
            c_prev = c_ref[b, h]
            n_prev = n_ref[b, h]
            mg = jnp.maximum(m_prev, g_col)
            w = jnp.exp(jnp.where(causal, a_row - mg, NEG_INF))
            inter = jnp.exp(m_prev - mg)
            qb = qh.astype(bf16)
            vb = vh.astype(bf16)
            wqk = w * lax.dot_general(qb, kh.astype(bf16), nt, preferred_element_type=jnp.float32)
            num = (jnp.dot(wqk.astype(bf16), vb, preferred_element_type=jnp.float32)
                   + inter * jnp.dot(qb, c_prev.astype(bf16), preferred_element_type=jnp.float32))
            den = (jnp.sum(wqk, axis=1, keepdims=True)
                   + inter * jnp.sum(qh * n_prev, axis=1, keepdims=True))
            h_ref[b, :, cols] = num / jnp.maximum(jnp.abs(den), jnp.exp(-(b_col + mg)))
            b_last = b_col[chunk - 1:chunk, :]
            m_last = b_last + mg[chunk - 1:chunk, :]
            f_last = jnp.exp(m_prev + b_last - m_last)
            kw = jnp.exp(a_col + b_last - m_last) * kh
            c_ref[b, h] = f_last * c_prev + jnp.dot(kw.T.astype(bf16), vb, preferred_element_type=jnp.float32)
            n_ref[b, h] = f_last * n_prev + jnp.sum(kw, axis=0, keepdims=True)
            m_ref[b, h] = jnp.broadcast_to(m_last, m_ref.shape[2:])


def _mlstm_prompt(q, k, v, i_pre, f_pre):
    bsz, s, hd = q.shape
    chunk = ML_CHUNK
    nc = s // chunk
    assert s % chunk == 0
    f32 = jnp.float32
    lf = jax.nn.log_sigmoid(f_pre).reshape(bsz, nc, chunk, ML_HEADS)
    b_cum = jnp.cumsum(lf, axis=2)
    a = i_pre.reshape(bsz, nc, chunk, ML_HEADS) - b_cum
    g = lax.cummax(a, axis=2)
    col = jnp.concatenate([a, b_cum, g], axis=-1).reshape(bsz, s, 3 * ML_HEADS)
    row = jnp.transpose(a.reshape(bsz, s, ML_HEADS), (0, 2, 1))
    blk = pl.BlockSpec((bsz, chunk, hd), lambda c: (0, c, 0))
    state = lambda shape: pl.BlockSpec(shape, lambda c: (0,) * len(shape))
    c_shape = (bsz, ML_HEADS, ML_DQK, ML_DV)
    n_shape = (bsz, ML_HEADS, 1, ML_DQK)
    h, c_f, n_f, m_f = pl.pallas_call(
        _mlstm_kernel,
        grid=(nc,),
        in_specs=[blk, blk, blk,
                  pl.BlockSpec((bsz, chunk, 3 * ML_HEADS), lambda c: (0, c, 0)),
                  pl.BlockSpec((bsz, ML_HEADS, chunk), lambda c: (0, 0, c))],
        out_specs=[blk, state(c_shape), state(n_shape), state(n_shape)],
        out_shape=[jax.ShapeDtypeStruct((bsz, s, hd), f32), jax.ShapeDtypeStruct(c_shape, f32),
                   jax.ShapeDtypeStruct(n_shape, f32), jax.ShapeDtypeStruct(n_shape, f32)],
        compiler_params=pltpu.CompilerParams(dimension_semantics=("arbitrary",)),
        name="mlstm_prompt",
    )(q, k, v, col, row)
    return h.reshape(bsz, s, ML_HEADS, ML_DV), c_f, n_f[:, :, 0, :], m_f[:, :, 0, 0]


def _decode_kernel(pt_ref, ql_ref, qr_ref, cn_ref, kn_ref, *refs, scale, pages):
    c_refs = refs[:pages]
    k_refs = refs[pages:2 * pages]
    o_ref = refs[2 * pages]
    m_s, l_s, acc_s = refs[2 * pages + 1:]
    g = pl.program_id(1)
    bf16 = jnp.bfloat16
    nt = (((1,), (1,)), ((), ()))
    ql = ql_ref[0]
    qr = qr_ref[0]

    @pl.when(g == 0)
    def _init():
        m_s[...] = jnp.full_like(m_s, NEG_INF)
        l_s[...] = jnp.zeros_like(l_s)
        acc_s[...] = jnp.zeros_like(acc_s)

    def update(s, c):
        m_new = jnp.maximum(m_s[...], jnp.max(s, axis=1, keepdims=True))
        alpha = jnp.exp(m_s[...] - m_new)
        p = jnp.exp(s - m_new)
        l_s[...] = alpha * l_s[...] + jnp.sum(p, axis=1, keepdims=True)
        acc_s[...] = alpha * acc_s[...] + jnp.dot(p.astype(bf16), c, preferred_element_type=jnp.float32)
        m_s[...] = m_new

    c = jnp.concatenate([r[0] for r in c_refs], axis=0).astype(bf16)
    krt = jnp.concatenate([r[0] for r in k_refs], axis=1).astype(bf16)
    s = (lax.dot_general(ql, c, nt, preferred_element_type=jnp.float32)
         + jnp.dot(qr, krt, preferred_element_type=jnp.float32)) * scale
    update(s, c)

    @pl.when(g == pl.num_programs(1) - 1)
    def _finish():
        cn = cn_ref[0].astype(bf16)
        kn = kn_ref[0].astype(bf16)
        sn = (lax.dot_general(ql, cn, nt, preferred_element_type=jnp.float32)
              + lax.dot_general(qr, kn, nt, preferred_element_type=jnp.float32)) * scale
        t_of_row = lax.broadcasted_iota(jnp.int32, sn.shape, 0) // MLA_HEADS
        key = lax.broadcasted_iota(jnp.int32, sn.shape, 1)
        update(jnp.where(key <= t_of_row, sn, NEG_INF), cn)
        o_ref[0] = acc_s[...] / l_s[...]


def _mla_decode(q_lat, q_rope, c_new, kr_new, cache_latent, cache_rope, page_table, scale,
                pages=DECODE_PAGES):
    b, rows, _ = q_lat.shape
    t = c_new.shape[1]
    n_pages = page_table.shape[1]
    assert n_pages % pages == 0 and t <= DEC_PAD
    bf16 = jnp.bfloat16
    pad = ((0, 0), (0, DEC_PAD - t), (0, 0))

    def page_spec(shape, i):
        return pl.BlockSpec((1,) + shape, lambda bi, g, pt: (pt[bi, g * pages + i], 0, 0))

    rope_t = jnp.swapaxes(cache_rope, 1, 2)

    grid_spec = pltpu.PrefetchScalarGridSpec(
        num_scalar_prefetch=1,
        grid=(b, n_pages // pages),
        in_specs=[
            pl.BlockSpec((1, rows, KV_LORA), lambda bi, g, pt: (bi, 0, 0)),
            pl.BlockSpec((1, rows, QK_ROPE), lambda bi, g, pt: (bi, 0, 0)),
            pl.BlockSpec((1, DEC_PAD, KV_LORA), lambda bi, g, pt: (bi, 0, 0)),
            pl.BlockSpec((1, DEC_PAD, QK_ROPE), lambda bi, g, pt: (bi, 0, 0)),
        ] + [page_spec((PAGE_SIZE, KV_LORA), i) for i in range(pages)]
          + [page_spec((QK_ROPE, PAGE_SIZE), i) for i in range(pages)],
        out_specs=pl.BlockSpec((1, rows, KV_LORA), lambda bi, g, pt: (bi, 0, 0)),
        scratch_shapes=[pltpu.VMEM((rows, 1), jnp.float32), pltpu.VMEM((rows, 1), jnp.float32),
                        pltpu.VMEM((rows, KV_LORA), jnp.float32)],
    )
    return pl.pallas_call(
        functools.partial(_decode_kernel, scale=scale, pages=pages),
        grid_spec=grid_spec,
        out_shape=jax.ShapeDtypeStruct((b, rows, KV_LORA), jnp.float32),
        compiler_params=pltpu.CompilerParams(
            dimension_semantics=("arbitrary", "arbitrary"),
            vmem_limit_bytes=FLASH_VMEM_LIMIT),
        name="mla_decode_attention",
    )(page_table, q_lat.astype(bf16), q_rope.astype(bf16), jnp.pad(c_new, pad), jnp.pad(kr_new, pad),
      *([cache_latent] * pages), *([rope_t] * pages))


def _matmul_kernel(x_ref, w_ref, o_ref):
    o_ref[...] = jnp.dot(x_ref[...].astype(jnp.bfloat16), w_ref[...].astype(jnp.bfloat16),
                         preferred_element_type=jnp.float32)


def _project(x, w, block_m=MATMUL_BLOCK_M, block_n=MATMUL_BLOCK_N):
    lead, k = x.shape[:-1], x.shape[-1]
    n = w.shape[1]
    rows = x.reshape(-1, k)
    m = rows.shape[0]
    assert m % block_m == 0
    n_pad = -n % LANES
    if n_pad:
        w = jnp.pad(w, ((0, 0), (0, n_pad)))
    n_full = n + n_pad
    tn = max(t for t in range(LANES, min(block_n, n_full) + 1, LANES) if n_full % t == 0)
    out = pl.pallas_call(
        _matmul_kernel,
        grid=(m // block_m, n_full // tn),
        in_specs=[pl.BlockSpec((block_m, k), lambda i, j: (i, 0)),
                  pl.BlockSpec((k, tn), lambda i, j: (0, j))],
        out_specs=pl.BlockSpec((block_m, tn), lambda i, j: (i, j)),
        out_shape=jax.ShapeDtypeStruct((m, n_full), jnp.float32),
        compiler_params=pltpu.CompilerParams(
            dimension_semantics=("arbitrary", "arbitrary"),
            vmem_limit_bytes=FLASH_VMEM_LIMIT),
        name="projection",
    )(rows, w)
    return out[:, :n].reshape(lead + (n,))


def _rmsnorm(x, g):
    xf = x.astype(jnp.float32)
    y = xf * lax.rsqrt(jnp.mean(xf * xf, axis=-1, keepdims=True) + NORM_EPS)
    return (y * g.astype(jnp.float32)).astype(x.dtype)


def _layernorm(x, g, b):
    xf = x.astype(jnp.float32)
    mu = jnp.mean(xf, axis=-1, keepdims=True)
    var = jnp.mean(jnp.square(xf - mu), axis=-1, keepdims=True)
    y = (xf - mu) * lax.rsqrt(var + NORM_EPS)
    return (y * g.astype(jnp.float32) + b.astype(jnp.float32)).astype(x.dtype)


def _rope(x, pos):
    half = QK_ROPE // 2
    inv = ROPE_THETA ** (-jnp.arange(half, dtype=jnp.float32) / half)
    ang = pos.astype(jnp.float32)[:, None] * inv[None, :]
    ang = ang.reshape((ang.shape[0],) + (1,) * (x.ndim - 3) + (half,))
    cos, sin = jnp.cos(ang), jnp.sin(ang)
    xf = x.astype(jnp.float32)
    x1, x2 = xf[..., :half], xf[..., half:]
    return jnp.concatenate([x1 * cos - x2 * sin, x1 * sin + x2 * cos], axis=-1).astype(x.dtype)


def _even_project(h, pos, w_in, q_norm, w_uq, kv_norm, b_i, b_f):
    B, S, _ = h.shape
    f32 = jnp.float32
    z = _project(h, w_in)
    cq, ckv, mq, mk, mv, mi, mf, mo = jnp.split(z, IN_SPLITS, axis=-1)
    q = _project(_rmsnorm(cq, q_norm), w_uq).reshape(B, S, MLA_HEADS, QK_NOPE + QK_ROPE)
    q_nope = q[..., :QK_NOPE]
    q_rope = _rope(q[..., QK_NOPE:], pos)
    c_kv = _rmsnorm(ckv[..., :KV_LORA], kv_norm)
    k_rope = _rope(ckv[..., KV_LORA:], pos)
    mq = mq.reshape(B, S, ML_HEADS, ML_DQK).astype(f32)
    mk = mk.reshape(B, S, ML_HEADS, ML_DQK).astype(f32) * (ML_DQK ** -0.5)
    mv = mv.reshape(B, S, ML_HEADS, ML_DV).astype(f32)
    i_pre = mi.astype(f32) + b_i.astype(f32)
    f_pre = mf.astype(f32) + b_f.astype(f32)
    return q_nope, q_rope, c_kv, k_rope, mq, mk, mv, i_pre, f_pre, mo


def _mla_prompt(q_nope, q_rope, c_kv, k_rope, w_uk, w_uv):
    B, S = q_nope.shape[0], q_nope.shape[1]
    bf16 = jnp.bfloat16
    scale = (QK_NOPE + QK_ROPE) ** -0.5
    k_nope = jnp.einsum('bsc,chd->bshd', c_kv, w_uk)
    v = jnp.einsum('bsc,chd->bshd', c_kv, w_uv)
    q = jnp.concatenate([q_nope, q_rope], axis=-1)
    k = jnp.concatenate([k_nope, jnp.broadcast_to(k_rope[:, :, None, :], (B, S, MLA_HEADS, QK_ROPE))], axis=-1)
    heads_major = lambda a: jnp.transpose(a, (0, 2, 1, 3)).astype(bf16)
    o = _flash_attention(heads_major(q), heads_major(k), heads_major(v), scale)
    return jnp.transpose(o, (0, 2, 1, 3))


def _mla_sample(q_nope, q_rope, c_new, kr_new, cache_latent, cache_rope, page_table, w_uk, w_uv):
    B, T = q_nope.shape[0], q_nope.shape[1]
    scale = (QK_NOPE + QK_ROPE) ** -0.5
    q_lat = jnp.einsum('bthd,chd->bthc', q_nope, w_uk)
    o_lat = _mla_decode(q_lat.reshape(B, T * MLA_HEADS, KV_LORA), q_rope.reshape(B, T * MLA_HEADS, QK_ROPE),
                        c_new, kr_new, cache_latent, cache_rope, page_table, scale)
    return jnp.einsum('bthc,chd->bthd', o_lat.reshape(B, T, MLA_HEADS, KV_LORA), w_uv)


def _mlstm_chunked(q, k, v, i_pre, f_pre, C0, n0, m0):
    B, S, H, _ = q.shape
    L = ML_CHUNK if S % ML_CHUNK == 0 else S
    nc = S // L
    logf = jax.nn.log_sigmoid(f_pre)

    def chunks(a):
        return jnp.moveaxis(a.reshape((B, nc, L) + a.shape[2:]), 1, 0)

    causal = jnp.tril(jnp.ones((L, L), dtype=bool))[None, :, :, None]

    def step(carry, inp):
        C, n, m = carry
        qc, kc, vc, ic, lfc = inp
        b = jnp.cumsum(lfc, axis=1)
        a = ic - b
        m_t = b + jnp.maximum(m[:, None, :], lax.cummax(a, axis=1))
        d_log = (b - m_t)[:, :, None, :] + a[:, None, :, :]
        w = jnp.exp(jnp.where(causal, d_log, -jnp.inf))
        inter = jnp.exp(m[:, None, :] + b - m_t)
        wqk = w * jnp.einsum('bthd,bshd->btsh', qc, kc)
        num = (jnp.einsum('btsh,bshv->bthv', wqk, vc)
               + inter[..., None] * jnp.einsum('bthd,bhdv->bthv', qc, C))
        den = wqk.sum(axis=2) + inter * jnp.einsum('bthd,bhd->bth', qc, n)
        h = num / jnp.maximum(jnp.abs(den), jnp.exp(-m_t))[..., None]
        m_last = m_t[:, -1]
        w_s = jnp.exp(a + b[:, -1:, :] - m_last[:, None, :])
        f_last = jnp.exp(m + b[:, -1] - m_last)
        C_new = f_last[..., None, None] * C + jnp.einsum('bsh,bshd,bshv->bhdv', w_s, kc, vc)
        n_new = f_last[..., None] * n + jnp.einsum('bsh,bshd->bhd', w_s, kc)
        return (C_new, n_new, m_last), h

    (C_f, n_f, m_f), hs = lax.scan(step, (C0, n0, m0),
                                   (chunks(q), chunks(k), chunks(v), chunks(i_pre), chunks(logf)))
    h = jnp.moveaxis(hs, 0, 1).reshape(B, S, H, v.shape[-1])
    return h, C_f, n_f, m_f


def _even_merge(att, h_ml, o_pre, ml_norm, w_out):
    B, S = att.shape[0], att.shape[1]
    hn = h_ml * lax.rsqrt(jnp.mean(h_ml * h_ml, axis=-1, keepdims=True) + NORM_EPS)
    y_ml = (hn.reshape(B, S, ML_HEADS * ML_DV) * ml_norm.astype(jnp.float32)
            * jax.nn.sigmoid(o_pre.astype(jnp.float32))).astype(att.dtype)
    mix = jnp.concatenate([att.reshape(B, S, MLA_HEADS * V_HEAD), y_ml], axis=-1)
    return _project(mix, w_out)


def _conformer_conv(h, buf, w_pw1, b_pw1, w_dw, b_dw, ln_g, ln_b, w_pw2, b_pw2):
    a = _project(h, w_pw1) + b_pw1
    g = a[..., :D_MODEL] * jax.nn.sigmoid(a[..., D_MODEL:])
    full = jnp.concatenate([buf.astype(g.dtype), g], axis=1)
    y = lax.conv_general_dilated(full, w_dw[:, None, :].astype(full.dtype), window_strides=(1,),
                                 padding='VALID', dimension_numbers=('NWC', 'WIO', 'NWC'),
                                 feature_group_count=D_MODEL) + b_dw
    y = jax.nn.silu(_layernorm(y, ln_g, ln_b))
    return _project(y, w_pw2) + b_pw2, full[:, -(CONV_W - 1):]


def kernel(x_prompt, x_sample, cache_mla_latent, cache_mla_rope, state_mlstm_C, state_mlstm_n, state_mlstm_m, state_conv, page_table, attn_norm, w_in, q_norm, w_uq, kv_norm, w_uk, w_uv, b_igate, b_fgate, ml_norm, w_out, conv_norm, w_pw1, b_pw1, w_dw, b_dw, conv_ln_g, conv_ln_b, w_pw2, b_pw2, ffn_norm, peer_wq, peer_keys, peer_u, peer_v, final_norm):
    f32 = jnp.float32
    bp, sp = x_prompt.shape[0], x_prompt.shape[1]
    bs, ss = x_sample.shape[0], x_sample.shape[1]
    n_p, n_s = bp * sp, bs * ss
    pos_p = jnp.arange(sp, dtype=jnp.int32)
    pos_s = PAST_LEN + jnp.arange(ss, dtype=jnp.int32)
    past = page_table.shape[1] * PAGE_SIZE
    xp, xs = x_prompt, x_sample
    lat_p, rope_p, lat_s, rope_s = [], [], [], []
    cp_l, np_l, mp_l, cs_l, ns_l, ms_l = [], [], [], [], [], []
    convp_l, convs_l = [], []
    for layer in range(DEPTH):
        j = layer // 2
        if layer % 2 == 0:
            (qn_p, qr_p, c_p, kr_p, mq_p, mk_p, mv_p, ig_p, fg_p, og_p) = _even_project(
                _rmsnorm(xp, attn_norm[j]), pos_p, w_in[j], q_norm[j], w_uq[j], kv_norm[j], b_igate[j], b_fgate[j])
            (qn_s, qr_s, c_s, kr_s, mq_s, mk_s, mv_s, ig_s, fg_s, og_s) = _even_project(
                _rmsnorm(xs, attn_norm[j]), pos_s, w_in[j], q_norm[j], w_uq[j], kv_norm[j], b_igate[j], b_fgate[j])
            att_p = _mla_prompt(qn_p, qr_p, c_p, kr_p, w_uk[j], w_uv[j])
            att_s = _mla_sample(qn_s, qr_s, c_s, kr_s, cache_mla_latent[j], cache_mla_rope[j], page_table,
                                w_uk[j], w_uv[j])
            h_p, C_p, n_p_, m_p = _mlstm_prompt(mq_p.reshape(bp, sp, -1), mk_p.reshape(bp, sp, -1),
                                                mv_p.reshape(bp, sp, -1), ig_p, fg_p)
            h_s, C_s, n_s_, m_s = _mlstm_chunked(mq_s, mk_s, mv_s, ig_s, fg_s,
                                                 state_mlstm_C[j].astype(f32), state_mlstm_n[j].astype(f32),
                                                 state_mlstm_m[j].astype(f32))
            xp = xp + _even_merge(att_p, h_p, og_p, ml_norm[j], w_out[j])
            xs = xs + _even_merge(att_s, h_s, og_s, ml_norm[j], w_out[j])
            lat_p.append(c_p)
            rope_p.append(kr_p)
            lat_s.append(c_s)
            rope_s.append(kr_s)
            cp_l.append(C_p)
            np_l.append(n_p_)
            mp_l.append(m_p)
            cs_l.append(C_s)
            ns_l.append(n_s_)
            ms_l.append(m_s)
        else:
            buf0 = jnp.zeros((bp, CONV_W - 1, D_MODEL), xp.dtype)
            out_p, buf_p = _conformer_conv(_rmsnorm(xp, conv_norm[j]), buf0, w_pw1[j], b_pw1[j], w_dw[j], b_dw[j],
                                           conv_ln_g[j], conv_ln_b[j], w_pw2[j], b_pw2[j])
            out_s, buf_s = _conformer_conv(_rmsnorm(xs, conv_norm[j]), state_conv[j], w_pw1[j], b_pw1[j], w_dw[j],
                                           b_dw[j], conv_ln_g[j], conv_ln_b[j], w_pw2[j], b_pw2[j])
            xp = xp + out_p
            xs = xs + out_s
            convp_l.append(buf_p)
            convs_l.append(buf_s)
        tables = _peer_tables(peer_wq[layer], peer_keys[layer], peer_u[layer], peer_v[layer])
        xp = _peer_residual(xp.reshape(n_p, D_MODEL), ffn_norm[layer], tables).reshape(bp, sp, D_MODEL)
        xs = _peer_residual(xs.reshape(n_s, D_MODEL), ffn_norm[layer], tables).reshape(bs, ss, D_MODEL)
    y_prompt = _rmsnorm(xp, final_norm)
    y_sample = _rmsnorm(xs, final_norm)
    return (y_prompt, y_sample, jnp.stack(lat_p), jnp.stack(rope_p), jnp.stack(lat_s), jnp.stack(rope_s),
            jnp.stack(cp_l), jnp.stack(np_l), jnp.stack(mp_l), jnp.stack(cs_l), jnp.stack(ns_l), jnp.stack(ms_l),
            jnp.stack(convp_l), jnp.stack(convs_l))
```

```python
import functools
import math

import jax
import jax.numpy as jnp
import numpy as np
from jax import lax
from jax.experimental import pallas as pl
from jax.experimental.pallas import tpu as pltpu

D_MODEL = 1024
DEPTH = 2
PAST_LEN = 8192
PAGE_SIZE = 128
NORM_EPS = 1e-6

MLA_HEADS = 8
Q_LORA = 384
KV_LORA = 256
QK_NOPE = 64
QK_ROPE = 32
V_HEAD = 64
ROPE_THETA = 10000.0
Q_BLOCK = 128

ML_HEADS = 4
ML_DQK = 128
ML_DV = 128
ML_CHUNK = 128
M_INIT = -1e30

IN_WIDTHS = (Q_LORA, KV_LORA + QK_ROPE, ML_HEADS * ML_DQK, ML_HEADS * ML_DQK,
             ML_HEADS * ML_DV, ML_HEADS, ML_HEADS, ML_HEADS * ML_DV)
IN_SPLITS = tuple(int(c) for c in np.cumsum(IN_WIDTHS)[:-1])

CONV_W = 31

PEER_HEADS = 8
N_KEYS = 128
N_EXPERTS = N_KEYS * N_KEYS
PEER_DKEY = 256
PEER_HALF = PEER_DKEY // 2
PEER_TOPK = 16

SUBLANES = 8
LANES = 128
BF16_ROWS = 16
MATMUL_BLOCK_M = 512
MATMUL_BLOCK_N = 1408
PEER_TOKENS = 512
PEER_EXPERT_CHUNK = 1024
PEER_GATE_TILES = 4
PEER_SORT_LANES = 256
PEER_VMEM_LIMIT = 56 * 1024 * 1024
FLASH_BLOCK = 512
FLASH_HEADS = 4
ATTN_VMEM_LIMIT = 48 * 1024 * 1024
DECODE_PAGES = 32
DEC_PAD = 16
FLASH_VMEM_LIMIT = 32 * 1024 * 1024

NEG_INF = float("-inf")
SQRT_HALF = math.sqrt(0.5)


def _peer_candidates(v1, v2):
    t = v1.shape[1]
    row = lax.broadcasted_iota(jnp.int32, (8, t), 0)
    slabs = [v1[0:8] + v2[0:1], v1[8:16] + v2[0:1], v1[0:8] + v2[1:2]]
    for b in range(2, 8):
        n_a = PEER_TOPK // (b + 1)
        slabs.append(jnp.where(row < n_a, v1[0:8] + v2[b:b + 1], NEG_INF))
    slabs.append(v1[0:1] + v2[8:16])
    return jnp.concatenate(slabs, axis=0)


def _extract_top(cur, n):
    vals = []
    for i in range(n):
        m = jnp.max(cur, axis=0, keepdims=True)
        vals.append(m)
        if i + 1 < n:
            cur = jnp.where(cur == m, NEG_INF, cur)
    return vals


def _merge_exchange_pairs(n):
    pairs = []
    p = 1
    while p < n:
        k = p
        while k >= 1:
            for j in range(k % p, n - k, 2 * k):
                for i in range(min(k, n - j - k)):
                    if (i + j) // (2 * p) == (i + j + k) // (2 * p):
                        pairs.append((i + j, i + j + k))
            k //= 2
        p *= 2
    return pairs


def _top_of_rows(s_ref, head, n, out_ref, lane_block):
    tiles = N_KEYS // SUBLANES
    for l0 in range(0, s_ref.shape[2], lane_block):
        lanes = slice(l0, l0 + lane_block)
        v = [s_ref[head, i * SUBLANES:(i + 1) * SUBLANES, lanes] for i in range(tiles)]
        for i, k in _merge_exchange_pairs(tiles):
            v[i], v[k] = jnp.maximum(v[i], v[k]), jnp.minimum(v[i], v[k])
        for r in range(n):
            m = jnp.max(v[0], axis=0, keepdims=True)
            out_ref[r:r + 1, lanes] = m
            if r + 1 < n:
                hit = v[0] == m
                for k in range(min(tiles, n) - 1 - r):
                    v[k] = jnp.where(hit, v[k + 1], v[k])


def _peer_prologue(x_ref, g_ref, wqt_ref, keys_ref, ht_s, s1_s, s2_s, e1_s, e2_s, tau_s, top_s):
    tb = x_ref.shape[0]
    x = x_ref[...]
    ms = jnp.mean(x * x, axis=-1, keepdims=True)
    hn = (x * lax.rsqrt(ms + NORM_EPS)) * g_ref[...]
    ht = hn.T.astype(jnp.bfloat16)
    ht_s[...] = ht
    qtb = jnp.dot(wqt_ref[...], ht, preferred_element_type=jnp.float32).astype(jnp.bfloat16)
    for h in range(PEER_HEADS):
        tops = []
        for p, s_ref in enumerate((s1_s, s2_s)):
            r0 = (h * 2 + p) * PEER_HALF
            s = jnp.dot(keys_ref[h, p], qtb[r0:r0 + PEER_HALF, :],
                        preferred_element_type=jnp.float32)
            s_ref[h] = s
            _top_of_rows(s_ref, h, PEER_TOPK, top_s.at[p], PEER_SORT_LANES)
            tops.append(top_s[p, 0:1, :])
        sc = _extract_top(_peer_candidates(top_s[0], top_s[1]), PEER_TOPK)
        z = jnp.zeros_like(sc[0])
        for k in range(PEER_TOPK):
            z = z + jnp.exp(sc[k] - sc[0])
        tau_s[h] = jnp.broadcast_to(sc[PEER_TOPK - 1], (SUBLANES, tb))
        e1_s[h] = jnp.exp(s1_s[h] - tops[0]) / z
        e2_s[h] = jnp.exp(s2_s[h] - tops[1])


def _peer_gate_chunk(j, s1_s, s2_s, e1_s, e2_s, tau_s, rep_s, at_ref, coef_ref, between=None):
    rows = at_ref.shape[0] // N_KEYS
    tb = at_ref.shape[1]
    e1_0 = pl.multiple_of(j * rows, SUBLANES)
    for h in range(PEER_HEADS):
        s1_blk = s1_s[h, pl.ds(e1_0, rows), :]
        e1_blk = e1_s[h, pl.ds(e1_0, rows), :]
        for r in range(rows):
            rep_s[0, h, r] = jnp.broadcast_to(s1_blk[r:r + 1, :], (SUBLANES, tb))
            rep_s[1, h, r] = jnp.broadcast_to(e1_blk[r:r + 1, :], (SUBLANES, tb))
    for r in range(rows):
        if between is not None:
            between(r)
        for t0 in range(0, N_KEYS // SUBLANES, PEER_GATE_TILES):
            gates = [None] * PEER_GATE_TILES
            for h in range(PEER_HEADS):
                s1r, e1r, tau = rep_s[0, h, r], rep_s[1, h, r], tau_s[h]
                for i in range(PEER_GATE_TILES):
                    r0 = (t0 + i) * SUBLANES
                    ssum = s1r + s2_s[h, r0:r0 + SUBLANES, :]
                    term = jnp.where(ssum >= tau, e1r * e2_s[h, r0:r0 + SUBLANES, :], 0.0)
                    gates[i] = term if gates[i] is None else gates[i] + term
            for i0 in range(0, PEER_GATE_TILES, BF16_ROWS // SUBLANES):
                halves = []
                for i in range(i0, i0 + BF16_ROWS // SUBLANES):
                    r0 = r * N_KEYS + (t0 + i) * SUBLANES
                    a = at_ref[r0:r0 + SUBLANES, :]
                    halves.append(gates[i] * ((0.5 * a) * (1.0 + lax.erf(a * SQRT_HALF))))
                row0 = r * N_KEYS + (t0 + i0) * SUBLANES
                coef_ref[row0:row0 + BF16_ROWS, :] = jnp.concatenate(halves, axis=0).astype(jnp.bfloat16)


def _peer_kernel(x_ref, g_ref, wqt_ref, keys_ref, u0_ref, u_ref, vt_ref, vtl_ref, o_ref,
                 ht_s, s1_s, s2_s, e1_s, e2_s, tau_s, top_s, rep_s,
                 at_a, at_b, coef_a, coef_b, acc_s):
    j = pl.program_id(1)
    n_chunks = pl.num_programs(1)

    @pl.when(j == 0)
    def _first():
        _peer_prologue(x_ref, g_ref, wqt_ref, keys_ref, ht_s, s1_s, s2_s, e1_s, e2_s, tau_s, top_s)
        at_a[...] = jnp.dot(u0_ref[...], ht_s[...], preferred_element_type=jnp.float32)
        coef_b[...] = jnp.zeros_like(coef_b)
        acc_s[...] = jnp.zeros_like(acc_s)

    def step(at_cur, at_nxt, coef_cur, coef_prv):
        chunk, tb = at_cur.shape
        d = acc_s.shape[0]

        def at_piece(mi, ni):
            m = slice(mi * chunk // 2, (mi + 1) * chunk // 2)
            n = slice(ni * tb // 2, (ni + 1) * tb // 2)
            at_nxt[m, n] = jnp.dot(u_ref[m, :], ht_s[:, n], preferred_element_type=jnp.float32)

        def acc_piece(mi, ni):
            m = slice(mi * d // 2, (mi + 1) * d // 2)
            n = slice(ni * tb // 2, (ni + 1) * tb // 2)
            acc_s[m, n] += jnp.dot(vt_ref[m, :], coef_prv[:, n], preferred_element_type=jnp.float32)

        pieces = [functools.partial(f, mi, ni) for ni in range(2) for mi in range(2) for f in (at_piece, acc_piece)]
        _peer_gate_chunk(j, s1_s, s2_s, e1_s, e2_s, tau_s, rep_s, at_cur, coef_cur,
                         between=lambda r: pieces[r]())

    @pl.when(j % 2 == 0)
    def _even():
        step(at_a, at_b, coef_a, coef_b)

    @pl.when(j % 2 == 1)
    def _odd():
        step(at_b, at_a, coef_b, coef_a)

    @pl.when(j == n_chunks - 1)
    def _last():
        acc = acc_s[...] + jnp.dot(vtl_ref[...], coef_b[...], preferred_element_type=jnp.float32)
        o_ref[...] = x_ref[...] + acc.T


def _peer_tables(w_q, sub_keys, u_tab, v_tab):
    bf16 = jnp.bfloat16
    return w_q.T.astype(bf16), sub_keys.astype(bf16), u_tab.astype(bf16), v_tab.T.astype(bf16)


def _peer_residual(x, g, tables, *, tokens=PEER_TOKENS, chunk=PEER_EXPERT_CHUNK):
    n, d = x.shape
    n_chunks = N_EXPERTS // chunk
    assert n % tokens == 0 and N_EXPERTS % chunk == 0 and chunk == SUBLANES * N_KEYS and n_chunks % 2 == 0
    wqt, keys, u_b, vt_b = tables
    f32 = jnp.float32
    once = pl.Buffered(1)
    per_head = pltpu.VMEM((PEER_HEADS, N_KEYS, tokens), f32)
    at_buf = pltpu.VMEM((chunk, tokens), f32)
    coef_buf = pltpu.VMEM((chunk, tokens), jnp.bfloat16)
    return pl.pallas_call(
        _peer_kernel,
        grid=(n // tokens, n_chunks),
        in_specs=[
            pl.BlockSpec((tokens, d), lambda i, j: (i, 0)),
            pl.BlockSpec((1, d), lambda i, j: (0, 0), pipeline_mode=once),
            pl.BlockSpec((PEER_HEADS * PEER_DKEY, d), lambda i, j: (0, 0), pipeline_mode=once),
            pl.BlockSpec((PEER_HEADS, 2, N_KEYS, PEER_HALF), lambda i, j: (0, 0, 0, 0), pipeline_mode=once),
            pl.BlockSpec((chunk, d), lambda i, j: (0, 0), pipeline_mode=once),
            pl.BlockSpec((chunk, d), lambda i, j: (jnp.minimum(j + 1, n_chunks - 1), 0)),
            pl.BlockSpec((d, chunk), lambda i, j: (0, jnp.maximum(j - 1, 0))),
            pl.BlockSpec((d, chunk), lambda i, j: (0, n_chunks - 1), pipeline_mode=once),
        ],
        out_specs=pl.BlockSpec((tokens, d), lambda i, j: (i, 0)),
        out_shape=jax.ShapeDtypeStruct((n, d), f32),
        scratch_shapes=[
            pltpu.VMEM((d, tokens), jnp.bfloat16),
            per_head, per_head, per_head, per_head,
            pltpu.VMEM((PEER_HEADS, SUBLANES, tokens), f32),
            pltpu.VMEM((2, PEER_TOPK, tokens), f32),
            pltpu.VMEM((2, PEER_HEADS, SUBLANES, SUBLANES, tokens), f32),
            at_buf, at_buf, coef_buf, coef_buf,
            pltpu.VMEM((d, tokens), f32),
        ],
        compiler_params=pltpu.CompilerParams(
            dimension_semantics=("arbitrary", "arbitrary"),
            vmem_limit_bytes=PEER_VMEM_LIMIT),
        name="peer",
    )(x, g.reshape(1, d), wqt, keys, u_b, u_b, vt_b, vt_b)


def _flash_kernel(q_ref, k_ref, v_ref, o_ref, *, scale, block):
    qi = pl.program_id(2)
    heads = q_ref.shape[1]

    def kv_block(kb, carries, diagonal):
        return tuple(head_block(hh, kb, carries[hh], diagonal) for hh in range(heads))

    def head_block(hh, kb, carry, diagonal):
        m, l, acc = carry
        q = q_ref[0, hh]
        k0 = pl.multiple_of(kb * block, block)
        k = k_ref[0, hh, pl.ds(k0, block), :]
        v = v_ref[0, hh, pl.ds(k0, block), :]
        s = lax.dot_general(q, k, (((1,), (1,)), ((), ())), preferred_element_type=jnp.float32) * scale
        if diagonal:
            row = lax.broadcasted_iota(jnp.int32, s.shape, 0)
            col = lax.broadcasted_iota(jnp.int32, s.shape, 1)
            s = jnp.where(row >= col, s, NEG_INF)
        m_new = jnp.maximum(m, jnp.max(s, axis=1, keepdims=True))
        alpha = jnp.exp(m - m_new)
        p = jnp.exp(s - m_new)
        l_new = alpha * l + jnp.sum(p, axis=1, keepdims=True)
        acc_new = alpha * acc + jnp.dot(p.astype(v.dtype), v, preferred_element_type=jnp.float32)
        return m_new, l_new, acc_new

    init = tuple((jnp.full((block, 1), NEG_INF, jnp.float32), jnp.zeros((block, 1), jnp.float32),
                  jnp.zeros((block, v_ref.shape[3]), jnp.float32)) for _ in range(heads))
    carries = lax.fori_loop(0, qi, lambda kb, c: kv_block(kb, c, False), init)
    for hh, (_, l, acc) in enumerate(kv_block(qi, carries, True)):
        o_ref[0, hh] = acc / l


def _flash_attention(q, k, v, scale, block=FLASH_BLOCK, heads=FLASH_HEADS):
    b, h, s, d_qk = q.shape
    d_v = v.shape[3]
    assert s % block == 0 and h % heads == 0
    return pl.pallas_call(
        functools.partial(_flash_kernel, scale=scale, block=block),
        grid=(b, h // heads, s // block),
        in_specs=[
            pl.BlockSpec((1, heads, block, d_qk), lambda bi, hi, qi: (bi, hi, qi, 0)),
            pl.BlockSpec((1, heads, s, d_qk), lambda bi, hi, qi: (bi, hi, 0, 0)),
            pl.BlockSpec((1, heads, s, d_v), lambda bi, hi, qi: (bi, hi, 0, 0)),
        ],
        out_specs=pl.BlockSpec((1, heads, block, d_v), lambda bi, hi, qi: (bi, hi, qi, 0)),
        out_shape=jax.ShapeDtypeStruct((b, h, s, d_v), jnp.float32),
        compiler_params=pltpu.CompilerParams(
            dimension_semantics=("arbitrary", "arbitrary", "arbitrary"),
            vmem_limit_bytes=ATTN_VMEM_LIMIT),
        name="mla_prompt_attention",
    )(q, k, v)


def _mlstm_kernel(q_ref, k_ref, v_ref, col_ref, row_ref, h_ref, c_ref, n_ref, m_ref):
    bsz, chunk, _ = q_ref.shape
    bf16 = jnp.bfloat16
    nt = (((1,), (1,)), ((), ()))

    @pl.when(pl.program_id(0) == 0)
    def _init():
        c_ref[...] = jnp.zeros_like(c_ref)
        n_ref[...] = jnp.zeros_like(n_ref)
        m_ref[...] = jnp.full_like(m_ref, M_INIT)

    t_idx = lax.broadcasted_iota(jnp.int32, (chunk, chunk), 0)
    s_idx = lax.broadcasted_iota(jnp.int32, (chunk, chunk), 1)
    causal = s_idx <= t_idx
    for b in range(bsz):
        for h in range(ML_HEADS):
            cols = slice(h * ML_DQK, (h + 1) * ML_DQK)
            qh, kh, vh = q_ref[b, :, cols], k_ref[b, :, cols], v_ref[b, :, cols]
            a_col = col_ref[b, :, h:h + 1]
            b_col = col_ref[b, :, ML_HEADS + h:ML_HEADS + h + 1]
            g_col = col_ref[b, :, 2 * ML_HEADS + h:2 * ML_HEADS + h + 1]
            a_row = row_ref[b, h:h + 1, :]
            m_prev = m_ref[b, h][:, 0:1]
            c_prev = c_ref[b, h]
            n_prev = n_ref[b, h]
            mg = jnp.maximum(m_prev, g_col)
            w = jnp.exp(jnp.where(causal, a_row - mg, NEG_INF))
            inter = jnp.exp(m_prev - mg)
            qb = qh.astype(bf16)
            vb = vh.astype(bf16)
            wqk = w * lax.dot_general(qb, kh.astype(bf16), nt, preferred_element_type=jnp.float32)
            num = (jnp.dot(wqk.astype(bf16), vb, preferred_element_type=jnp.float32)
                   + inter * jnp.dot(qb, c_prev.astype(bf16), preferred_element_type=jnp.float32))
            den = (jnp.sum(wqk, axis=1, keepdims=True)
                   + inter * jnp.sum(qh * n_prev, axis=1, keepdims=True))
            h_ref[b, :, cols] = num / jnp.maximum(jnp.abs(den), jnp.exp(-(b_col + mg)))
            b_last = b_col[chunk - 1:chunk, :]
            m_last = b_last + mg[chunk - 1:chunk, :]
            f_last = jnp.exp(m_prev + b_last - m_last)
            kw = jnp.exp(a_col + b_last - m_last) * kh
            c_ref[b, h] = f_last * c_prev + jnp.dot(kw.T.astype(bf16), vb, preferred_element_type=jnp.float32)
            n_ref[b, h] = f_last * n_prev + jnp.sum(kw, axis=0, keepdims=True)
            m_ref[b, h] = jnp.broadcast_to(m_last, m_ref.shape[2:])


def _mlstm_prompt(q, k, v, i_pre, f_pre):
    bsz, s, hd = q.shape
    chunk = ML_CHUNK
    nc = s // chunk
    assert s % chunk == 0
    f32 = jnp.float32
    lf = jax.nn.log_sigmoid(f_pre).reshape(bsz, nc, chunk, ML_HEADS)
    b_cum = jnp.cumsum(lf, axis=2)
    a = i_pre.reshape(bsz, nc, chunk, ML_HEADS) - b_cum
    g = lax.cummax(a, axis=2)
    col = jnp.concatenate([a, b_cum, g], axis=-1).reshape(bsz, s, 3 * ML_HEADS)
    row = jnp.transpose(a.reshape(bsz, s, ML_HEADS), (0, 2, 1))
    blk = pl.BlockSpec((bsz, chunk, hd), lambda c: (0, c, 0))
    state = lambda shape: pl.BlockSpec(shape, lambda c: (0,) * len(shape))
    c_shape = (bsz, ML_HEADS, ML_DQK, ML_DV)
    n_shape = (bsz, ML_HEADS, 1, ML_DQK)
    h, c_f, n_f, m_f = pl.pallas_call(
        _mlstm_kernel,
        grid=(nc,),
        in_specs=[blk, blk, blk,
                  pl.BlockSpec((bsz, chunk, 3 * ML_HEADS), lambda c: (0, c, 0)),
                  pl.BlockSpec((bsz, ML_HEADS, chunk), lambda c: (0, 0, c))],
        out_specs=[blk, state(c_shape), state(n_shape), state(n_shape)],
        out_shape=[jax.ShapeDtypeStruct((bsz, s, hd), f32), jax.ShapeDtypeStruct(c_shape, f32),
                   jax.ShapeDtypeStruct(n_shape, f32), jax.ShapeDtypeStruct(n_shape, f32)],
        compiler_params=pltpu.CompilerParams(dimension_semantics=("arbitrary",)),
        name="mlstm_prompt",
    )(q, k, v, col, row)
    return h.reshape(bsz, s, ML_HEADS, ML_DV), c_f, n_f[:, :, 0, :], m_f[:, :, 0, 0]


def _decode_kernel(pt_ref, ql_ref, qr_ref, cn_ref, kn_ref, *refs, scale, pages):
    c_refs = refs[:pages]
    k_refs = refs[pages:2 * pages]
    o_ref = refs[2 * pages]
    m_s, l_s, acc_s = refs[2 * pages + 1:]
    g = pl.program_id(1)
    bf16 = jnp.bfloat16
    nt = (((1,), (1,)), ((), ()))
    ql = ql_ref[0]
    qr = qr_ref[0]

    @pl.when(g == 0)
    def _init():
        m_s[...] = jnp.full_like(m_s, NEG_INF)
        l_s[...] = jnp.zeros_like(l_s)
        acc_s[...] = jnp.zeros_like(acc_s)

    def update(s, c):
        m_new = jnp.maximum(m_s[...], jnp.max(s, axis=1, keepdims=True))
        alpha = jnp.exp(m_s[...] - m_new)
        p = jnp.exp(s - m_new)
        l_s[...] = alpha * l_s[...] + jnp.sum(p, axis=1, keepdims=True)
        acc_s[...] = alpha * acc_s[...] + jnp.dot(p.astype(bf16), c, preferred_element_type=jnp.float32)
        m_s[...] = m_new

    c = jnp.concatenate([r[0] for r in c_refs], axis=0).astype(bf16)
    krt = jnp.concatenate([r[0] for r in k_refs], axis=1).astype(bf16)
    s = (lax.dot_general(ql, c, nt, preferred_element_type=jnp.float32)
         + jnp.dot(qr, krt, preferred_element_type=jnp.float32)) * scale
    update(s, c)

    @pl.when(g == pl.num_programs(1) - 1)
    def _finish():
        cn = cn_ref[0].astype(bf16)
        kn = kn_ref[0].astype(bf16)
        sn = (lax.dot_general(ql, cn, nt, preferred_element_type=jnp.float32)
              + lax.dot_general(qr, kn, nt, preferred_element_type=jnp.float32)) * scale
        t_of_row = lax.broadcasted_iota(jnp.int32, sn.shape, 0) // MLA_HEADS
        key = lax.broadcasted_iota(jnp.int32, sn.shape, 1)
        update(jnp.where(key <= t_of_row, sn, NEG_INF), cn)
        o_ref[0] = acc_s[...] / l_s[...]


def _mla_decode(q_lat, q_rope, c_new, kr_new, cache_latent, cache_rope, page_table, scale,
                pages=DECODE_PAGES):
    b, rows, _ = q_lat.shape
    t = c_new.shape[1]
    n_pages = page_table.shape[1]
    assert n_pages % pages == 0 and t <= DEC_PAD
    bf16 = jnp.bfloat16
    pad = ((0, 0), (0, DEC_PAD - t), (0, 0))

    def page_spec(shape, i):
        return pl.BlockSpec((1,) + shape, lambda bi, g, pt: (pt[bi, g * pages + i], 0, 0))

    rope_t = jnp.swapaxes(cache_rope, 1, 2)

    grid_spec = pltpu.PrefetchScalarGridSpec(
        num_scalar_prefetch=1,
        grid=(b, n_pages // pages),
        in_specs=[
            pl.BlockSpec((1, rows, KV_LORA), lambda bi, g, pt: (bi, 0, 0)),
            pl.BlockSpec((1, rows, QK_ROPE), lambda bi, g, pt: (bi, 0, 0)),
            pl.BlockSpec((1, DEC_PAD, KV_LORA), lambda bi, g, pt: (bi, 0, 0)),
            pl.BlockSpec((1, DEC_PAD, QK_ROPE), lambda bi, g, pt: (bi, 0, 0)),
        ] + [page_spec((PAGE_SIZE, KV_LORA), i) for i in range(pages)]
          + [page_spec((QK_ROPE, PAGE_SIZE), i) for i in range(pages)],
        out_specs=pl.BlockSpec((1, rows, KV_LORA), lambda bi, g, pt: (bi, 0, 0)),
        scratch_shapes=[pltpu.VMEM((rows, 1), jnp.float32), pltpu.VMEM((rows, 1), jnp.float32),
                        pltpu.VMEM((rows, KV_LORA), jnp.float32)],
    )
    return pl.pallas_call(
        functools.partial(_decode_kernel, scale=scale, pages=pages),
        grid_spec=grid_spec,
        out_shape=jax.ShapeDtypeStruct((b, rows, KV_LORA), jnp.float32),
        compiler_params=pltpu.CompilerParams(
            dimension_semantics=("arbitrary", "arbitrary"),
            vmem_limit_bytes=FLASH_VMEM_LIMIT),
        name="mla_decode_attention",
    )(page_table, q_lat.astype(bf16), q_rope.astype(bf16), jnp.pad(c_new, pad), jnp.pad(kr_new, pad),
      *([cache_latent] * pages), *([rope_t] * pages))


def _matmul_kernel(x_ref, w_ref, o_ref):
    o_ref[...] = jnp.dot(x_ref[...].astype(jnp.bfloat16), w_ref[...].astype(jnp.bfloat16),
                         preferred_element_type=jnp.float32)


def _project(x, w, block_m=MATMUL_BLOCK_M, block_n=MATMUL_BLOCK_N):
    lead, k = x.shape[:-1], x.shape[-1]
    n = w.shape[1]
    rows = x.reshape(-1, k)
    m = rows.shape[0]
    assert m % block_m == 0
    n_pad = -n % LANES
    if n_pad:
        w = jnp.pad(w, ((0, 0), (0, n_pad)))
    n_full = n + n_pad
    tn = max(t for t in range(LANES, min(block_n, n_full) + 1, LANES) if n_full % t == 0)
    out = pl.pallas_call(
        _matmul_kernel,
        grid=(m // block_m, n_full // tn),
        in_specs=[pl.BlockSpec((block_m, k), lambda i, j: (i, 0)),
                  pl.BlockSpec((k, tn), lambda i, j: (0, j))],
        out_specs=pl.BlockSpec((block_m, tn), lambda i, j: (i, j)),
        out_shape=jax.ShapeDtypeStruct((m, n_full), jnp.float32),
        compiler_params=pltpu.CompilerParams(
            dimension_semantics=("arbitrary", "arbitrary"),
            vmem_limit_bytes=FLASH_VMEM_LIMIT),
        name="projection",
    )(rows, w)
    return out[:, :n].reshape(lead + (n,))


def _rmsnorm(x, g):
    xf = x.astype(jnp.float32)
    y = xf * lax.rsqrt(jnp.mean(xf * xf, axis=-1, keepdims=True) + NORM_EPS)
    return (y * g.astype(jnp.float32)).astype(x.dtype)


def _layernorm(x, g, b):
    xf = x.astype(jnp.float32)
    mu = jnp.mean(xf, axis=-1, keepdims=True)
    var = jnp.mean(jnp.square(xf - mu), axis=-1, keepdims=True)
    y = (xf - mu) * lax.rsqrt(var + NORM_EPS)
    return (y * g.astype(jnp.float32) + b.astype(jnp.float32)).astype(x.dtype)


def _rope(x, pos):
    half = QK_ROPE // 2
    inv = ROPE_THETA ** (-jnp.arange(half, dtype=jnp.float32) / half)
    ang = pos.astype(jnp.float32)[:, None] * inv[None, :]
    ang = ang.reshape((ang.shape[0],) + (1,) * (x.ndim - 3) + (half,))
    cos, sin = jnp.cos(ang), jnp.sin(ang)
    xf = x.astype(jnp.float32)
    x1, x2 = xf[..., :half], xf[..., half:]
    return jnp.concatenate([x1 * cos - x2 * sin, x1 * sin + x2 * cos], axis=-1).astype(x.dtype)


def _even_project(h, pos, w_in, q_norm, w_uq, kv_norm, b_i, b_f):
    B, S, _ = h.shape
    f32 = jnp.float32
    z = _project(h, w_in)
    cq, ckv, mq, mk, mv, mi, mf, mo = jnp.split(z, IN_SPLITS, axis=-1)
    q = _project(_rmsnorm(cq, q_norm), w_uq).reshape(B, S, MLA_HEADS, QK_NOPE + QK_ROPE)
    q_nope = q[..., :QK_NOPE]
    q_rope = _rope(q[..., QK_NOPE:], pos)
    c_kv = _rmsnorm(ckv[..., :KV_LORA], kv_norm)
    k_rope = _rope(ckv[..., KV_LORA:], pos)
    mq = mq.reshape(B, S, ML_HEADS, ML_DQK).astype(f32)
    mk = mk.reshape(B, S, ML_HEADS, ML_DQK).astype(f32) * (ML_DQK ** -0.5)
    mv = mv.reshape(B, S, ML_HEADS, ML_DV).astype(f32)
    i_pre = mi.astype(f32) + b_i.astype(f32)
    f_pre = mf.astype(f32) + b_f.astype(f32)
    return q_nope, q_rope, c_kv, k_rope, mq, mk, mv, i_pre, f_pre, mo


def _mla_prompt(q_nope, q_rope, c_kv, k_rope, w_uk, w_uv):
    B, S = q_nope.shape[0], q_nope.shape[1]
    bf16 = jnp.bfloat16
    scale = (QK_NOPE + QK_ROPE) ** -0.5
    k_nope = jnp.einsum('bsc,chd->bshd', c_kv, w_uk)
    v = jnp.einsum('bsc,chd->bshd', c_kv, w_uv)
    q = jnp.concatenate([q_nope, q_rope], axis=-1)
    k = jnp.concatenate([k_nope, jnp.broadcast_to(k_rope[:, :, None, :], (B, S, MLA_HEADS, QK_ROPE))], axis=-1)
    heads_major = lambda a: jnp.transpose(a, (0, 2, 1, 3)).astype(bf16)
    o = _flash_attention(heads_major(q), heads_major(k), heads_major(v), scale)
    return jnp.transpose(o, (0, 2, 1, 3))


def _mla_sample(q_nope, q_rope, c_new, kr_new, cache_latent, cache_rope, page_table, w_uk, w_uv):
    B, T = q_nope.shape[0], q_nope.shape[1]
    scale = (QK_NOPE + QK_ROPE) ** -0.5
    q_lat = jnp.einsum('bthd,chd->bthc', q_nope, w_uk)
    o_lat = _mla_decode(q_lat.reshape(B, T * MLA_HEADS, KV_LORA), q_rope.reshape(B, T * MLA_HEADS, QK_ROPE),
                        c_new, kr_new, cache_latent, cache_rope, page_table, scale)
    return jnp.einsum('bthc,chd->bthd', o_lat.reshape(B, T, MLA_HEADS, KV_LORA), w_uv)


def _mlstm_chunked(q, k, v, i_pre, f_pre, C0, n0, m0):
    B, S, H, _ = q.shape
    L = ML_CHUNK if S % ML_CHUNK == 0 else S
    nc = S // L
    logf = jax.nn.log_sigmoid(f_pre)

    def chunks(a):
        return jnp.moveaxis(a.reshape((B, nc, L) + a.shape[2:]), 1, 0)

    causal = jnp.tril(jnp.ones((L, L), dtype=bool))[None, :, :, None]

    def step(carry, inp):
        C, n, m = carry
        qc, kc, vc, ic, lfc = inp
        b = jnp.cumsum(lfc, axis=1)
        a = ic - b
        m_t = b + jnp.maximum(m[:, None, :], lax.cummax(a, axis=1))
        d_log = (b - m_t)[:, :, None, :] + a[:, None, :, :]
        w = jnp.exp(jnp.where(causal, d_log, -jnp.inf))
        inter = jnp.exp(m[:, None, :] + b - m_t)
        wqk = w * jnp.einsum('bthd,bshd->btsh', qc, kc)
        num = (jnp.einsum('btsh,bshv->bthv', wqk, vc)
               + inter[..., None] * jnp.einsum('bthd,bhdv->bthv', qc, C))
        den = wqk.sum(axis=2) + inter * jnp.einsum('bthd,bhd->bth', qc, n)
        h = num / jnp.maximum(jnp.abs(den), jnp.exp(-m_t))[..., None]
        m_last = m_t[:, -1]
        w_s = jnp.exp(a + b[:, -1:, :] - m_last[:, None, :])
        f_last = jnp.exp(m + b[:, -1] - m_last)
        C_new = f_last[..., None, None] * C + jnp.einsum('bsh,bshd,bshv->bhdv', w_s, kc, vc)
        n_new = f_last[..., None] * n + jnp.einsum('bsh,bshd->bhd', w_s, kc)
        return (C_new, n_new, m_last), h

    (C_f, n_f, m_f), hs = lax.scan(step, (C0, n0, m0),
                                   (chunks(q), chunks(k), chunks(v), chunks(i_pre), chunks(logf)))
    h = jnp.moveaxis(hs, 0, 1).reshape(B, S, H, v.shape[-1])
    return h, C_f, n_f, m_f


def _even_merge(att, h_ml, o_pre, ml_norm, w_out):
    B, S = att.shape[0], att.shape[1]
    hn = h_ml * lax.rsqrt(jnp.mean(h_ml * h_ml, axis=-1, keepdims=True) + NORM_EPS)
    y_ml = (hn.reshape(B, S, ML_HEADS * ML_DV) * ml_norm.astype(jnp.float32)
            * jax.nn.sigmoid(o_pre.astype(jnp.float32))).astype(att.dtype)
    mix = jnp.concatenate([att.reshape(B, S, MLA_HEADS * V_HEAD), y_ml], axis=-1)
    return _project(mix, w_out)


def _conformer_conv(h, buf, w_pw1, b_pw1, w_dw, b_dw, ln_g, ln_b, w_pw2, b_pw2):
    a = _project(h, w_pw1) + b_pw1
    g = a[..., :D_MODEL] * jax.nn.sigmoid(a[..., D_MODEL:])
    full = jnp.concatenate([buf.astype(g.dtype), g], axis=1)
    y = lax.conv_general_dilated(full, w_dw[:, None, :].astype(full.dtype), window_strides=(1,),
                                 padding='VALID', dimension_numbers=('NWC', 'WIO', 'NWC'),
                                 feature_group_count=D_MODEL) + b_dw
    y = jax.nn.silu(_layernorm(y, ln_g, ln_b))
    return _project(y, w_pw2) + b_pw2, full[:, -(CONV_W - 1):]


def kernel(x_prompt, x_sample, cache_mla_latent, cache_mla_rope, state_mlstm_C, state_mlstm_n, state_mlstm_m, state_conv, page_table, attn_norm, w_in, q_norm, w_uq, kv_norm, w_uk, w_uv, b_igate, b_fgate, ml_norm, w_out, conv_norm, w_pw1, b_pw1, w_dw, b_dw, conv_ln_g, conv_ln_b, w_pw2, b_pw2, ffn_norm, peer_wq, peer_keys, peer_u, peer_v, final_norm):
    f32 = jnp.float32
    bp, sp = x_prompt.shape[0], x_prompt.shape[1]
    bs, ss = x_sample.shape[0], x_sample.shape[1]
    n_p, n_s = bp * sp, bs * ss
    pos_p = jnp.arange(sp, dtype=jnp.int32)
    pos_s = PAST_LEN + jnp.arange(ss, dtype=jnp.int32)
    past = page_table.shape[1] * PAGE_SIZE
    xp, xs = x_prompt, x_sample
    lat_p, rope_p, lat_s, rope_s = [], [], [], []
    cp_l, np_l, mp_l, cs_l, ns_l, ms_l = [], [], [], [], [], []
    convp_l, convs_l = [], []
    for layer in range(DEPTH):
        j = layer // 2
        if layer % 2 == 0:
            (qn_p, qr_p, c_p, kr_p, mq_p, mk_p, mv_p, ig_p, fg_p, og_p) = _even_project(
                _rmsnorm(xp, attn_norm[j]), pos_p, w_in[j], q_norm[j], w_uq[j], kv_norm[j], b_igate[j], b_fgate[j])
            (qn_s, qr_s, c_s, kr_s, mq_s, mk_s, mv_s, ig_s, fg_s, og_s) = _even_project(
                _rmsnorm(xs, attn_norm[j]), pos_s, w_in[j], q_norm[j], w_uq[j], kv_norm[j], b_igate[j], b_fgate[j])
            att_p = _mla_prompt(qn_p, qr_p, c_p, kr_p, w_uk[j], w_uv[j])
            att_s = _mla_sample(qn_s, qr_s, c_s, kr_s, cache_mla_latent[j], cache_mla_rope[j], page_table,
                                w_uk[j], w_uv[j])
            h_p, C_p, n_p_, m_p = _mlstm_prompt(mq_p.reshape(bp, sp, -1), mk_p.reshape(bp, sp, -1),
                                                mv_p.reshape(bp, sp, -1), ig_p, fg_p)
            h_s, C_s, n_s_, m_s = _mlstm_chunked(mq_s, mk_s, mv_s, ig_s, fg_s,
                                                 state_mlstm_C[j].astype(f32), state_mlstm_n[j].astype(f32),
                                                 state_mlstm_m[j].astype(f32))
            xp = xp + _even_merge(att_p, h_p, og_p, ml_norm[j], w_out[j])
            xs = xs + _even_merge(att_s, h_s, og_s, ml_norm[j], w_out[j])
            lat_p.append(c_p)
            rope_p.append(kr_p)
            lat_s.append(c_s)
            rope_s.append(kr_s)
            cp_l.append(C_p)
            np_l.append(n_p_)
            mp_l.append(m_p)
            cs_l.append(C_s)
            ns_l.append(n_s_)
            ms_l.append(m_s)
        else:
            buf0 = jnp.zeros((bp, CONV_W - 1, D_MODEL), xp.dtype)
            out_p, buf_p = _conformer_conv(_rmsnorm(xp, conv_norm[j]), buf0, w_pw1[j], b_pw1[j], w_dw[j], b_dw[j],
                                           conv_ln_g[j], conv_ln_b[j], w_pw2[j], b_pw2[j])
            out_s, buf_s = _conformer_conv(_rmsnorm(xs, conv_norm[j]), state_conv[j], w_pw1[j], b_pw1[j], w_dw[j],
                                           b_dw[j], conv_ln_g[j], conv_ln_b[j], w_pw2[j], b_pw2[j])
            xp = xp + out_p
            xs = xs + out_s
            convp_l.append(buf_p)
            convs_l.append(buf_s)
        tables = _peer_tables(peer_wq[layer], peer_keys[layer], peer_u[layer], peer_v[layer])
        xp = _peer_residual(xp.reshape(n_p, D_MODEL), ffn_norm[layer], tables).reshape(bp, sp, D_MODEL)
        xs = _peer_residual(xs.reshape(n_s, D_MODEL), ffn_norm[layer], tables).reshape(bs, ss, D_MODEL)
    y_prompt = _rmsnorm(xp, final_norm)
    y_sample = _rmsnorm(xs, final_norm)
    return (y_prompt, y_sample, jnp.stack(lat_p), jnp.stack(rope_p), jnp.stack(lat_s), jnp.stack(rope_s),
            jnp.stack(cp_l), jnp.stack(np_l), jnp.stack(mp_l), jnp.stack(cs_l), jnp.stack(ns_l), jnp.stack(ms_l),
            jnp.stack(convp_l), jnp.stack(convs_l))
```
